```python
import jax, jax.numpy as jnp
from jax import lax
import numpy as np

D_MODEL = 2048
BATCH = 4
SEQ = 2048
DEPTH = 4
DEC_BATCH = 128
DEC_SEQ = 8
PAST_LEN = 16384
PAGE_SIZE = 128

D_A = D_MODEL // 2
RW_HEAD = 64
RW_HEADS = D_A // RW_HEAD
W_LORA = max(32, int(round(1.8 * D_A ** 0.5 / 32)) * 32)
A_LORA = max(32, int(round(1.8 * D_A ** 0.5 / 32)) * 32)
V_LORA = max(32, int(round(1.3 * D_A ** 0.5 / 32)) * 32)
G_LORA = max(32, int(round(0.6 * D_A ** 0.8 / 32)) * 32)
RW_LN_EPS = 64e-5
D_B = D_MODEL - D_A
HG_DK = 128
HG_DV = 128
HG_HEADS = D_B // HG_DV
HG_CHUNK = 32
F_FLOOR = 1e-30
N_IN = 3 * D_A + 2 * HG_HEADS * HG_DK + 2 * HG_HEADS * HG_DV
N_EXPERTS = 32
TOP_K = 4
D_FF = D_MODEL
SWIGLU_LIMIT = 7.0
SWIGLU_ALPHA = 1.702
MOE_BLOCK = 128
DN_ALPHA = (2 * DEPTH) ** 0.25
DN_BETA = (8 * DEPTH) ** -0.25
LN_EPS = 1e-5

kernel_name = 'hymba_rwkv7_hgrn2_moe_step'

F32 = jnp.float32


def layer_norm(x, g, b):
    xf = x.astype(F32)
    mu = jnp.mean(xf, -1, keepdims=True)
    var = jnp.mean(jnp.square(xf - mu), -1, keepdims=True)
    return ((xf - mu) * lax.rsqrt(var + LN_EPS) * g + b).astype(x.dtype)


def shift_rows(t, prev):
    return jnp.concatenate([prev[:, None], t[:, :-1]], axis=1)


def rwkv7_recurrence(r, w_log, k, v, a_vec, b_vec, S0):
    decay = jnp.exp(-jnp.exp(w_log.astype(F32)))
    seq_first = lambda t: jnp.moveaxis(t.astype(F32), 1, 0)

    def step(S, inp):
        r_t, d_t, k_t, v_t, a_t, b_t = inp
        sa = jnp.einsum('bhvk,bhk->bhv', S, a_t)
        S = S * d_t[:, :, None, :] + sa[..., None] * b_t[:, :, None, :] + v_t[..., None] * k_t[:, :, None, :]
        return S, jnp.einsum('bhvk,bhk->bhv', S, r_t)

    S, y = lax.scan(step, S0.astype(F32), tuple(seq_first(t) for t in (r, decay, k, v, a_vec, b_vec)))
    return jnp.moveaxis(y, 0, 1), S


def hgrn2_chunked(q, k, v, logf, S0):
    B, L, H, DK = q.shape
    DV = v.shape[-1]
    C = min(HG_CHUNK, L)
    pad = (-L) % C
    n = (L + pad) // C

    def prep(t):
        t = jnp.pad(t.astype(F32), ((0, 0), (0, pad), (0, 0), (0, 0)))
        return t.reshape(B, n, C, H, t.shape[-1]).transpose(1, 0, 3, 2, 4)

    causal = jnp.tril(jnp.ones((C, C), bool))[:, :, None]

    def step(S, inp):
        q_c, k_c, v_c, g_c = inp
        b = jnp.cumsum(g_c, axis=2)
        o_inter = jnp.einsum('bhcd,bhde->bhce', q_c * jnp.exp(b), S)
        diff = b[:, :, :, None, :] - b[:, :, None, :, :]
        dec = jnp.where(causal, jnp.exp(jnp.where(causal, diff, 0.0)), 0.0)
        A = jnp.sum(q_c[:, :, :, None, :] * k_c[:, :, None, :, :] * dec, axis=-1)
        o = o_inter + jnp.einsum('bhts,bhse->bhte', A, v_c)
        b_last = b[:, :, -1:, :]
        S = jnp.exp(b_last[:, :, 0, :])[..., None] * S + jnp.einsum('bhsd,bhse->bhde', k_c * jnp.exp(b_last - b), v_c)
        return S, o

    S, o = lax.scan(step, S0.astype(F32), (prep(q), prep(k), prep(v), prep(logf)))
    o = o.transpose(1, 0, 3, 2, 4).reshape(B, n * C, H, DV)[:, :L]
    return o, S


def rwkv7_group(xm, prev, zA, S0, v_first, l, P):
    B, L, _ = xm.shape
    heads = lambda t: t.reshape(B, L, RW_HEADS, RW_HEAD)
    zA = zA + (shift_rows(zA, prev @ P['w_in'][l][:, :3 * D_A]) - zA) * P['mu_rkv'][l]
    r, k, v = jnp.split(zA, 3, axis=-1)
    xx = shift_rows(xm, prev) - xm
    mu = P['mu_in'][l]
    xw, xa, xg = xm + xx * mu[0], xm + xx * mu[1], xm + xx * mu[2]
    w_log = -jax.nn.softplus(-(P['w0'][l] + jnp.tanh(xw @ P['w1'][l]) @ P['w2'][l])) - 0.5
    if l == 0:
        v_first = v
    else:
        xv = xm + xx * P['mu_vg'][l - 1]
        v = v + (v_first - v) * jax.nn.sigmoid(P['v0'][l - 1] + (xv @ P['v1'][l - 1]) @ P['v2'][l - 1])
    a = jax.nn.sigmoid(P['a0'][l] + (xa @ P['a1'][l]) @ P['a2'][l])
    g = jax.nn.sigmoid(xg @ P['g1'][l]) @ P['g2'][l]
    kk = heads(k * P['k_k'][l]).astype(F32)
    kk = kk / jnp.maximum(jnp.sqrt(jnp.sum(kk * kk, -1, keepdims=True)), 1e-12)
    k = k * (1 + (a - 1) * P['k_a'][l])
    y, S = rwkv7_recurrence(heads(r), heads(w_log), heads(k), heads(v), -kk, kk * heads(a).astype(F32), S0)
    mu_y = jnp.mean(y, -1, keepdims=True)
    var_y = jnp.mean(jnp.square(y - mu_y), -1, keepdims=True)
    y = ((y - mu_y) * lax.rsqrt(var_y + RW_LN_EPS)).reshape(B, L, D_A) * P['lnx_g'][l] + P['lnx_b'][l]
    bonus = (jnp.sum(heads(r) * heads(k) * P['r_k'][l], -1, keepdims=True) * heads(v)).reshape(B, L, D_A)
    return ((y + bonus) * g).astype(xm.dtype), v_first, S.astype(S0.dtype)


def hgrn2_group(zB, S0, lb, l, P):
    B, L, _ = zB.shape
    nk, nv = HG_HEADS * HG_DK, HG_HEADS * HG_DV
    zq, zf, zi, zg = jnp.split(zB, [nk, 2 * nk, 2 * nk + nv], axis=-1)
    q = jax.nn.silu(zq).reshape(B, L, HG_HEADS, HG_DK)
    zf32 = zf.astype(F32)
    f = lb + (1 - lb) * jax.nn.sigmoid(zf32)
    kf = ((1 - lb) * jax.nn.sigmoid(-zf32)).reshape(B, L, HG_HEADS, HG_DK)
    logf = jnp.log(jnp.maximum(f, F_FLOOR)).reshape(B, L, HG_HEADS, HG_DK)
    o, S = hgrn2_chunked(q, kf, zi.reshape(B, L, HG_HEADS, HG_DV), logf, S0)
    o = o * lax.rsqrt(jnp.mean(o * o, -1, keepdims=True) + LN_EPS) * P['hg_norm_w'][l]
    return (o.reshape(B, L, D_B) * jax.nn.silu(zg)).astype(zB.dtype), S.astype(S0.dtype)


def moe(h, l, P):
    T, D = h.shape
    logits = (h @ P['w_router'][l] + P['b_router'][l]).astype(F32)
    top_val, top_idx = lax.top_k(logits, TOP_K)
    gates = jax.nn.softmax(top_val, axis=-1).astype(h.dtype)
    TK = T * TOP_K
    flat_e = top_idx.reshape(TK)
    order = jnp.argsort(flat_e)
    e_sorted = flat_e[order]
    tok_sorted = (jnp.arange(TK, dtype=jnp.int32) // TOP_K)[order]
    w_sorted = gates.reshape(TK)[order]
    counts = jnp.bincount(flat_e, length=N_EXPERTS)
    padded = (counts + MOE_BLOCK - 1) // MOE_BLOCK * MOE_BLOCK
    pad_end = jnp.cumsum(padded)
    dest = (pad_end - padded)[e_sorted] + jnp.arange(TK) - (jnp.cumsum(counts) - counts)[e_sorted]
    n_blocks = -(-(TK + N_EXPERTS * (MOE_BLOCK - 1)) // MOE_BLOCK)
    rows = n_blocks * MOE_BLOCK
    tok_pad = jnp.full((rows,), T, jnp.int32).at[dest].set(tok_sorted)
    w_pad = jnp.zeros((rows,), h.dtype).at[dest].set(w_sorted)
    block_e = jnp.minimum(jnp.searchsorted(pad_end, jnp.arange(n_blocks) * MOE_BLOCK, side='right'), N_EXPERTS - 1)
    xb = jnp.concatenate([h, jnp.zeros((1, D), h.dtype)])[tok_pad].reshape(n_blocks, MOE_BLOCK, D)
    w_gu, b_gu, w_dn, b_dn = P['w_gu'][l], P['b_gu'][l], P['w_down'][l], P['b_down'][l]

    def expert_block(args):
        xe, e = args
        gu = xe @ w_gu[e] + b_gu[e]
        glu = jnp.minimum(gu[:, :D_FF], SWIGLU_LIMIT)
        lin = jnp.clip(gu[:, D_FF:], -SWIGLU_LIMIT, SWIGLU_LIMIT)
        return ((lin + 1) * glu * jax.nn.sigmoid(SWIGLU_ALPHA * glu)) @ w_dn[e] + b_dn[e]

    yb = lax.map(expert_block, (xb, block_e)).reshape(rows, D)
    return jnp.zeros((T + 1, D), h.dtype).at[tok_pad].add(yb * w_pad[:, None])[:T]


def trunk(x, c, s_rw, s_hg, s_sh, P):
    B, L, D = x.shape
    p = jax.nn.softmax(P['hg_lower'].astype(F32), axis=0)
    lb_all = jnp.cumsum(p, axis=0) - p[0]
    v_first = None
    rw_new, hg_new, sh_new = [], [], []
    for l in range(DEPTH):
        mod = jax.nn.silu(c) @ P['w_ada'][l] + P['b_ada'][l]
        sh1, sc1, gt1, sh2, sc2, gt2 = [m[:, None] for m in jnp.split(mod, 6, axis=-1)]
        xm = x * (1 + sc1) + sh1
        z = xm @ P['w_in'][l]
        oA, v_first, S_rw = rwkv7_group(xm, s_sh[l], z[..., :3 * D_A], s_rw[l], v_first, l, P)
        oB, S_hg = hgrn2_group(z[..., 3 * D_A:], s_hg[l], lb_all[l], l, P)
        mix = jnp.concatenate([oA, oB], axis=-1) @ P['w_out'][l]
        h = layer_norm(DN_ALPHA * x + (1 + gt1) * mix, P['ln1_g'][l], P['ln1_b'][l])
        hm = h * (1 + sc2) + sh2
        ffn = moe(hm.reshape(B * L, D), l, P).reshape(B, L, D)
        x = layer_norm(DN_ALPHA * h + (1 + gt2) * ffn, P['ln2_g'][l], P['ln2_b'][l])
        rw_new.append(S_rw)
        hg_new.append(S_hg)
        sh_new.append(xm[:, -1])
    return x, jnp.stack(rw_new), jnp.stack(hg_new), jnp.stack(sh_new)


def setup_inputs(seed: int = 0) -> dict:
    key = jax.random.key(seed)
    ks = iter(jax.random.split(key, 64))

    def nrm(shape, scale):
        return jax.random.normal(next(ks), shape, jnp.float32) * scale

    def uni(shape):
        return jax.random.uniform(next(ks), shape, jnp.float32)

    D, L1 = D_MODEL, DEPTH - 1
    nk = HG_HEADS * HG_DK
    return {
        'x_prompt': nrm((BATCH, SEQ, D), 1.0),
        'x_sample': nrm((DEC_BATCH, DEC_SEQ, D), 1.0),
        'c_prompt': nrm((BATCH, D), 1.0),
        'c_sample': nrm((DEC_BATCH, D), 1.0),
        'state_rwkv': nrm((DEPTH, DEC_BATCH, RW_HEADS, RW_HEAD, RW_HEAD), 0.5),
        'state_hgrn': nrm((DEPTH, DEC_BATCH, HG_HEADS, HG_DK, HG_DV), 0.5),
        'state_shift': nrm((DEPTH, DEC_BATCH, D), 1.0),
        'w_ada': nrm((DEPTH, D, 6 * D), 0.1 * D ** -0.5),
        'b_ada': nrm((DEPTH, 6 * D), 0.01),
        'w_in': nrm((DEPTH, D, N_IN), D ** -0.5),
        'w_out': nrm((DEPTH, D, D), DN_BETA * D ** -0.5),
        'mu_rkv': uni((DEPTH, 3 * D_A)),
        'mu_in': uni((DEPTH, 3, D)),
        'w0': nrm((DEPTH, D_A), 0.5),
        'w1': nrm((DEPTH, D, W_LORA), D ** -0.5),
        'w2': nrm((DEPTH, W_LORA, D_A), W_LORA ** -0.5),
        'a0': nrm((DEPTH, D_A), 0.1),
        'a1': nrm((DEPTH, D, A_LORA), D ** -0.5),
        'a2': nrm((DEPTH, A_LORA, D_A), A_LORA ** -0.5),
        'v0': nrm((L1, D_A), 0.1),
        'v1': nrm((L1, D, V_LORA), D ** -0.5),
        'v2': nrm((L1, V_LORA, D_A), V_LORA ** -0.5),
        'mu_vg': uni((L1, D)),
        'g1': nrm((DEPTH, D, G_LORA), D ** -0.5),
        'g2': nrm((DEPTH, G_LORA, D_A), G_LORA ** -0.5),
        'k_k': 0.85 + nrm((DEPTH, D_A), 0.05),
        'k_a': 1.0 + nrm((DEPTH, D_A), 0.05),
        'r_k': nrm((DEPTH, RW_HEADS, RW_HEAD), 0.1),
        'lnx_g': 1.0 + nrm((DEPTH, D_A), 0.01),
        'lnx_b': nrm((DEPTH, D_A), 0.01),
        'hg_lower': nrm((DEPTH, nk), 0.5),
        'hg_norm_w': 1.0 + nrm((DEPTH, HG_DV), 0.01),
        'ln1_g': 1.0 + nrm((DEPTH, D), 0.01),
        'ln1_b': nrm((DEPTH, D), 0.01),
        'ln2_g': 1.0 + nrm((DEPTH, D), 0.01),
        'ln2_b': nrm((DEPTH, D), 0.01),
        'w_router': nrm((DEPTH, D, N_EXPERTS), D ** -0.5),
        'b_router': nrm((DEPTH, N_EXPERTS), 0.01),
        'w_gu': nrm((DEPTH, N_EXPERTS, D, 2 * D_FF), D ** -0.5),
        'b_gu': nrm((DEPTH, N_EXPERTS, 2 * D_FF), 0.01),
        'w_down': nrm((DEPTH, N_EXPERTS, D_FF, D), DN_BETA * D_FF ** -0.5),
        'b_down': nrm((DEPTH, N_EXPERTS, D), 0.01),
    }


def reference(x_prompt, x_sample, c_prompt, c_sample, state_rwkv, state_hgrn, state_shift,
              w_ada, b_ada, w_in, w_out, mu_rkv, mu_in, w0, w1, w2, a0, a1, a2,
              v0, v1, v2, mu_vg, g1, g2, k_k, k_a, r_k, lnx_g, lnx_b, hg_lower, hg_norm_w,
              ln1_g, ln1_b, ln2_g, ln2_b, w_router, b_router, w_gu, b_gu, w_down, b_down):
    P = dict(w_ada=w_ada, b_ada=b_ada, w_in=w_in, w_out=w_out, mu_rkv=mu_rkv, mu_in=mu_in,
             w0=w0, w1=w1, w2=w2, a0=a0, a1=a1, a2=a2, v0=v0, v1=v1, v2=v2, mu_vg=mu_vg,
             g1=g1, g2=g2, k_k=k_k, k_a=k_a, r_k=r_k, lnx_g=lnx_g, lnx_b=lnx_b,
             hg_lower=hg_lower, hg_norm_w=hg_norm_w, ln1_g=ln1_g, ln1_b=ln1_b, ln2_g=ln2_g, ln2_b=ln2_b,
             w_router=w_router, b_router=b_router, w_gu=w_gu, b_gu=b_gu, w_down=w_down, b_down=b_down)
    B = x_prompt.shape[0]
    dt = x_prompt.dtype
    rw0 = jnp.zeros((DEPTH, B, RW_HEADS, RW_HEAD, RW_HEAD), dt)
    hg0 = jnp.zeros((DEPTH, B, HG_HEADS, HG_DK, HG_DV), dt)
    sh0 = jnp.zeros((DEPTH, B, D_MODEL), dt)
    y_prompt, rw_p, hg_p, sh_p = trunk(x_prompt, c_prompt, rw0, hg0, sh0, P)
    y_sample, rw_s, hg_s, sh_s = trunk(x_sample, c_sample, state_rwkv, state_hgrn, state_shift, P)
    return (y_prompt, y_sample, rw_p, rw_s, hg_p, hg_s, sh_p, sh_s)
```

```python
import functools

import jax
import jax.numpy as jnp
from jax import lax
from jax.experimental import pallas as pl
from jax.experimental.pallas import tpu as pltpu

F32 = jnp.float32
BF16 = jnp.bfloat16

TOP_K = 4
SWIGLU_LIMIT = 7.0
SWIGLU_ALPHA = 1.702
LN_EPS = 1e-5
RW_LN_EPS = 64e-5
F_FLOOR = 1e-30

LANES = 128
SUBLANES = 8
GROUP = SUBLANES
VMEM_LIMIT = 56 * 1024 * 1024

MM_TM = 1024
MM_TN = 1024
EW_GROUPS = 64
MOE_TM = 512
MOE_TF = 512
RW_SPLIT = 2
HG_SPLIT = 4
SEQ_TB = 32


def _divisor(n, pref):
    d = min(n, pref)
    while n % d:
        d -= 1
    return d


def _params(sem):
    return pltpu.CompilerParams(dimension_semantics=sem, vmem_limit_bytes=VMEM_LIMIT)


def _mm_kernel(x_ref, w_ref, o_ref, wbf_ref):
    @pl.when(pl.program_id(1) == 0)
    def _():
        wbf_ref[...] = w_ref[...].astype(BF16)

    o_ref[...] = jnp.dot(x_ref[...].astype(BF16), wbf_ref[...], preferred_element_type=F32)


def _mm_bias_kernel(x_ref, w_ref, b_ref, o_ref, wbf_ref):
    @pl.when(pl.program_id(1) == 0)
    def _():
        wbf_ref[...] = w_ref[...].astype(BF16)

    o_ref[...] = jnp.dot(x_ref[...].astype(BF16), wbf_ref[...], preferred_element_type=F32) + b_ref[...]


def mm(x, w, layer=None, bias=None, *, n_cols=None, name="mm"):
    M, K = x.shape
    N = w.shape[-1] if n_cols is None else n_cols
    tm = MM_TM if M % MM_TM == 0 else M
    tn = MM_TN if N % MM_TN == 0 else N
    grid = (N // tn, M // tm)
    if layer is None:
        w_spec = pl.BlockSpec((K, tn), lambda j, i: (0, j))
    else:
        w_spec = pl.BlockSpec((None, K, tn), lambda j, i: (layer, 0, j))
    in_specs = [pl.BlockSpec((tm, K), lambda j, i: (i, 0)), w_spec]
    args = [x, w]
    kern = _mm_kernel
    if bias is not None:
        if layer is None:
            in_specs.append(pl.BlockSpec((1, tn), lambda j, i: (0, j)))
        else:
            in_specs.append(pl.BlockSpec((None, 1, tn), lambda j, i: (layer, 0, j)))
        args.append(bias)
        kern = _mm_bias_kernel
    return pl.pallas_call(
        kern,
        grid=grid,
        in_specs=in_specs,
        out_specs=pl.BlockSpec((tm, tn), lambda j, i: (i, j)),
        out_shape=jax.ShapeDtypeStruct((M, N), F32),
        scratch_shapes=[pltpu.VMEM((K, tn), BF16)],
        compiler_params=_params(("arbitrary", "arbitrary")),
        name=name,
    )(*args)


def _modulate_kernel(x_ref, sc_ref, sh_ref, o_ref):
    gb = sc_ref.shape[0]
    d = x_ref.shape[-1]
    x = x_ref[...].reshape(gb, GROUP, d)
    xm = x * (1.0 + sc_ref[...]) + sh_ref[...]
    o_ref[...] = xm.reshape(gb * GROUP, d).astype(o_ref.dtype)


def modulate(x, mod, sc_idx, sh_idx):
    T, D = x.shape
    G = T // GROUP
    gb = _divisor(G, EW_GROUPS)
    return pl.pallas_call(
        _modulate_kernel,
        grid=(G // gb,),
        in_specs=[
            pl.BlockSpec((gb * GROUP, D), lambda i: (i, 0)),
            pl.BlockSpec((gb, 1, D), lambda i: (i, 0, sc_idx)),
            pl.BlockSpec((gb, 1, D), lambda i: (i, 0, sh_idx)),
        ],
        out_specs=pl.BlockSpec((gb * GROUP, D), lambda i: (i, 0)),
        out_shape=jax.ShapeDtypeStruct((T, D), BF16),
        compiler_params=_params(("arbitrary",)),
        name="modulate",
    )(x, mod, mod)


def _deepnorm(x_ref, y_ref, gt_ref, g_ref, b_ref, alpha):
    gb = gt_ref.shape[0]
    d = x_ref.shape[-1]
    x = x_ref[...].reshape(gb, GROUP, d)
    y = y_ref[...].reshape(gb, GROUP, d)
    u = alpha * x + (1.0 + gt_ref[...]) * y
    mu = jnp.mean(u, axis=-1, keepdims=True)
    uc = u - mu
    var = jnp.mean(uc * uc, axis=-1, keepdims=True)
    return uc * lax.rsqrt(var + LN_EPS) * g_ref[...] + b_ref[...]


def _ln_router_kernel(x_ref, y_ref, gt_ref, g_ref, b_ref, sc_ref, sh_ref, wr_ref, br_ref,
                      h_ref, hm_ref, lg_ref, *, alpha):
    gb = gt_ref.shape[0]
    d = x_ref.shape[-1]
    h = _deepnorm(x_ref, y_ref, gt_ref, g_ref, b_ref, alpha)
    hm = (h * (1.0 + sc_ref[...]) + sh_ref[...]).reshape(gb * GROUP, d)
    h_ref[...] = h.reshape(gb * GROUP, d)
    hm_ref[...] = hm.astype(BF16)
    lg_ref[...] = jnp.dot(hm, wr_ref[...], preferred_element_type=F32,
                          precision=lax.Precision.HIGHEST) + br_ref[...]


def ln_router(x, y, mod, gt_idx, sc_idx, sh_idx, g, b, w_router, b_router, layer, alpha):
    T, D = x.shape
    G = T // GROUP
    gb = _divisor(G, EW_GROUPS)
    E = w_router.shape[-1]
    rows = pl.BlockSpec((gb * GROUP, D), lambda i: (i, 0))
    vec = pl.BlockSpec((None, 1, D), lambda i: (layer, 0, 0))
    return pl.pallas_call(
        functools.partial(_ln_router_kernel, alpha=alpha),
        grid=(G // gb,),
        in_specs=[
            rows, rows,
            pl.BlockSpec((gb, 1, D), lambda i: (i, 0, gt_idx)),
            vec, vec,
            pl.BlockSpec((gb, 1, D), lambda i: (i, 0, sc_idx)),
            pl.BlockSpec((gb, 1, D), lambda i: (i, 0, sh_idx)),
            pl.BlockSpec((None, D, E), lambda i: (layer, 0, 0)),
            pl.BlockSpec((None, 1, E), lambda i: (layer, 0, 0)),
        ],
        out_specs=[rows, rows, pl.BlockSpec((gb * GROUP, E), lambda i: (i, 0))],
        out_shape=[
            jax.ShapeDtypeStruct((T, D), F32),
            jax.ShapeDtypeStruct((T, D), BF16),
            jax.ShapeDtypeStruct((T, E), F32),
        ],
        compiler_params=_params(("arbitrary",)),
        name="ln_router",
    )(x, y, mod, g, b, mod, mod, w_router, b_router)


def _ln_kernel(x_ref, y_ref, gt_ref, g_ref, b_ref, o_ref, *, alpha):
    gb = gt_ref.shape[0]
    d = x_ref.shape[-1]
    o_ref[...] = _deepnorm(x_ref, y_ref, gt_ref, g_ref, b_ref, alpha).reshape(gb * GROUP, d)


def ln(x, y, mod, gt_idx, g, b, layer, alpha):
    T, D = x.shape
    G = T // GROUP
    gb = _divisor(G, EW_GROUPS)
    rows = pl.BlockSpec((gb * GROUP, D), lambda i: (i, 0))
    vec = pl.BlockSpec((None, 1, D), lambda i: (layer, 0, 0))
    return pl.pallas_call(
        functools.partial(_ln_kernel, alpha=alpha),
        grid=(G // gb,),
        in_specs=[rows, rows, pl.BlockSpec((gb, 1, D), lambda i: (i, 0, gt_idx)), vec, vec],
        out_specs=rows,
        out_shape=jax.ShapeDtypeStruct((T, D), F32),
        compiler_params=_params(("arbitrary",)),
        name="ln",
    )(x, y, mod, g, b)


def _rwkv_kernel(a_ref, d_ref, b_ref, k_ref, r_ref, v_ref, s0_ref, y_ref, s_ref):
    @pl.when(pl.program_id(1) == 0)
    def _():
        s_ref[...] = s0_ref[...]

    n_steps = a_ref.shape[0]
    n_rows = v_ref.shape[1]

    def step(t, carry):
        a = a_ref[t]
        d = d_ref[t]
        b = b_ref[t]
        k = k_ref[t]
        r = r_ref[t]

        def row(v, c):
            s = s_ref[v]
            sa = jnp.sum(s * a, axis=0, keepdims=True)
            vv = v_ref[t, pl.ds(v, 1), :]
            s = s * d + sa * b + vv * k
            s_ref[v] = s
            y_ref[t, pl.ds(v, 1), :] = jnp.sum(s * r, axis=0, keepdims=True)
            return c

        return lax.fori_loop(0, n_rows, row, carry, unroll=True)

    lax.fori_loop(0, n_steps, step, 0)


def rwkv_scan(a, d, b, k, r, v, s0):
    L, K, NL = a.shape
    VR = v.shape[1]
    tb = SEQ_TB if L % SEQ_TB == 0 else L
    vec = pl.BlockSpec((tb, K, LANES), lambda g, t: (t, 0, g))
    val = pl.BlockSpec((tb, VR, LANES), lambda g, t: (t, 0, g))
    st = pl.BlockSpec((VR, K, LANES), lambda g, t: (0, 0, g))
    return pl.pallas_call(
        _rwkv_kernel,
        grid=(NL // LANES, L // tb),
        in_specs=[vec, vec, vec, vec, vec, val, st],
        out_specs=[val, st],
        out_shape=[jax.ShapeDtypeStruct((L, VR, NL), F32), jax.ShapeDtypeStruct((VR, K, NL), F32)],
        compiler_params=_params(("arbitrary", "arbitrary")),
        name="rwkv_scan",
    )(a, d, b, k, r, v, s0)


def _hgrn_kernel(f_ref, k_ref, q_ref, v_ref, s0_ref, o_ref, s_ref):
    @pl.when(pl.program_id(1) == 0)
    def _():
        s_ref[...] = s0_ref[...]

    n_steps = f_ref.shape[0]
    n_kblocks = f_ref.shape[1] // SUBLANES
    n_cols = v_ref.shape[1]

    def step(t, carry):
        def kblock(kb, acc):
            rows = pl.ds(pl.multiple_of(kb * SUBLANES, SUBLANES), SUBLANES)
            f = f_ref[t, rows, :]
            k = k_ref[t, rows, :]
            q = q_ref[t, rows, :]
            new = []
            for v in range(n_cols):
                s = f * s_ref[v, rows, :] + k * v_ref[t, pl.ds(v, 1), :]
                s_ref[v, rows, :] = s
                new.append(acc[v] + s * q)
            return tuple(new)

        zero = jnp.zeros((SUBLANES, LANES), F32)
        acc = lax.fori_loop(0, n_kblocks, kblock, tuple(zero for _ in range(n_cols)))
        for v in range(n_cols):
            o_ref[t, pl.ds(v, 1), :] = jnp.sum(acc[v], axis=0, keepdims=True)
        return carry

    lax.fori_loop(0, n_steps, step, 0)


def hgrn_scan(f, k, q, v, s0):
    L, K, NL = f.shape
    VC = v.shape[1]
    tb = SEQ_TB if L % SEQ_TB == 0 else L
    vec = pl.BlockSpec((tb, K, LANES), lambda g, t: (t, 0, g))
    val = pl.BlockSpec((tb, VC, LANES), lambda g, t: (t, 0, g))
    st = pl.BlockSpec((VC, K, LANES), lambda g, t: (0, 0, g))
    return pl.pallas_call(
        _hgrn_kernel,
        grid=(NL // LANES, L // tb),
        in_specs=[vec, vec, vec, val, st],
        out_specs=[val, st],
        out_shape=[jax.ShapeDtypeStruct((L, VC, NL), F32), jax.ShapeDtypeStruct((VC, K, NL), F32)],
        compiler_params=_params(("arbitrary", "arbitrary")),
        name="hgrn_scan",
    )(f, k, q, v, s0)


def to_lanes(x, split):
    B, L, H, C = x.shape
    xt = jnp.transpose(x, (1, 3, 0, 2))[..., None]
    return jnp.broadcast_to(xt, (L, C, B, H, split)).reshape(L, C, B * H * split)


def vals_to_lanes(v, split):
    B, L, H, C = v.shape
    vt = jnp.transpose(v.reshape(B, L, H, split, C // split), (1, 4, 0, 2, 3))
    return vt.reshape(L, C // split, B * H * split)


def vals_from_lanes(y, B, H, split):
    L, cs, _ = y.shape
    yt = jnp.transpose(y.reshape(L, cs, B, H, split), (2, 0, 3, 4, 1))
    return yt.reshape(B, L, H * split * cs)


def _moe_kernel(be_ref, nu_ref, x_ref, wp_ref, wg_ref, wl_ref, bg_ref, bl_ref, wd_ref, bd_ref,
                o_ref, acc_ref):
    i = pl.program_id(0)
    j = pl.program_id(1)
    last = pl.num_programs(1) - 1
    active = i < nu_ref[0]

    @pl.when(active)
    def _():
        x = x_ref[...]
        g = jnp.dot(x, wg_ref[...].astype(BF16), preferred_element_type=F32) + bg_ref[...]
        lin = jnp.dot(x, wl_ref[...].astype(BF16), preferred_element_type=F32) + bl_ref[...]
        glu = jnp.minimum(g, SWIGLU_LIMIT)
        lin = jnp.clip(lin, -SWIGLU_LIMIT, SWIGLU_LIMIT)
        act = (lin + 1.0) * glu * jax.nn.sigmoid(SWIGLU_ALPHA * glu)
        part = jnp.dot(act.astype(BF16), wd_ref[...].astype(BF16), preferred_element_type=F32)

        @pl.when(j == 0)
        def _():
            acc_ref[...] = part

        @pl.when(j > 0)
        def _():
            acc_ref[...] += part

        @pl.when(j == last)
        def _():
            o_ref[...] = (acc_ref[...] + bd_ref[...]) * wp_ref[...]

    @pl.when(jnp.logical_and(jnp.logical_not(active), j == last))
    def _():
        o_ref[...] = jnp.zeros_like(o_ref)


def moe_experts(block_e, n_used, xb, w_pad, w_gu, b_gu, w_down, b_down, layer):
    rows, D = xb.shape
    d_ff = w_down.shape[2]
    tm, tf = MOE_TM, MOE_TF
    nff = d_ff // tf
    n_blocks = rows // tm

    def hold(i, j, nu):
        return jnp.where(i < nu[0], j, nff - 1)

    def row_blk(i, nu):
        return jnp.minimum(i, jnp.maximum(nu[0] - 1, 0))

    grid_spec = pltpu.PrefetchScalarGridSpec(
        num_scalar_prefetch=2,
        grid=(n_blocks, nff),
        in_specs=[
            pl.BlockSpec((tm, D), lambda i, j, be, nu: (row_blk(i, nu), 0)),
            pl.BlockSpec((tm, 1), lambda i, j, be, nu: (row_blk(i, nu), 0)),
            pl.BlockSpec((None, None, D, tf), lambda i, j, be, nu: (layer, be[i], 0, hold(i, j, nu))),
            pl.BlockSpec((None, None, D, tf), lambda i, j, be, nu: (layer, be[i], 0, hold(i, j, nu) + nff)),
            pl.BlockSpec((None, None, 1, tf), lambda i, j, be, nu: (layer, be[i], 0, hold(i, j, nu))),
            pl.BlockSpec((None, None, 1, tf), lambda i, j, be, nu: (layer, be[i], 0, hold(i, j, nu) + nff)),
            pl.BlockSpec((None, None, tf, D), lambda i, j, be, nu: (layer, be[i], hold(i, j, nu), 0)),
            pl.BlockSpec((None, None, 1, D), lambda i, j, be, nu: (layer, be[i], 0, 0)),
        ],
        out_specs=pl.BlockSpec((tm, D), lambda i, j, be, nu: (i, 0)),
        scratch_shapes=[pltpu.VMEM((tm, D), F32)],
    )
    return pl.pallas_call(
        _moe_kernel,
        grid_spec=grid_spec,
        out_shape=jax.ShapeDtypeStruct((rows, D), F32),
        compiler_params=_params(("arbitrary", "arbitrary")),
        name="moe_experts",
    )(block_e, n_used, xb, w_pad, w_gu, w_gu, b_gu, b_gu, w_down, b_down)


def moe(hm, logits, w_gu, b_gu, w_down, b_down, layer):
    T, D = hm.shape
    E = logits.shape[-1]
    blk = MOE_TM
    top_val, top_idx = lax.top_k(logits, TOP_K)
    gates = jax.nn.softmax(top_val, axis=-1)
    TK = T * TOP_K
    flat_e = top_idx.reshape(TK).astype(jnp.int32)
    order = jnp.argsort(flat_e)
    e_sorted = flat_e[order]
    tok_sorted = (order // TOP_K).astype(jnp.int32)
    w_sorted = gates.reshape(TK)[order]
    counts = jnp.bincount(flat_e, length=E).astype(jnp.int32)
    padded = (counts + blk - 1) // blk * blk
    pad_end = jnp.cumsum(padded)
    dest = (pad_end - padded)[e_sorted] + jnp.arange(TK, dtype=jnp.int32) - (jnp.cumsum(counts) - counts)[e_sorted]
    n_blocks = -(-(TK + E * (blk - 1)) // blk)
    rows = n_blocks * blk
    tok_pad = jnp.full((rows,), T, jnp.int32).at[dest].set(tok_sorted)
    w_pad = jnp.zeros((rows,), F32).at[dest].set(w_sorted)
    block_e = jnp.minimum(jnp.searchsorted(pad_end, jnp.arange(n_blocks, dtype=jnp.int32) * blk, side='right'),
                          E - 1).astype(jnp.int32)
    n_used = (pad_end[-1] // blk).astype(jnp.int32).reshape(1)
    xb = jnp.concatenate([hm, jnp.zeros((1, D), hm.dtype)])[tok_pad]
    yb = moe_experts(block_e, n_used, xb, w_pad[:, None], w_gu, b_gu, w_down, b_down, layer)
    pos = jnp.zeros((TK,), jnp.int32).at[order].set(dest)
    return jnp.sum(yb[pos].reshape(T, TOP_K, D), axis=1)


def kernel(x_prompt, x_sample, c_prompt, c_sample, state_rwkv, state_hgrn, state_shift, w_ada, b_ada, w_in, w_out, mu_rkv, mu_in, w0, w1, w2, a0, a1, a2, v0, v1, v2, mu_vg, g1, g2, k_k, k_a, r_k, lnx_g, lnx_b, hg_lower, hg_norm_w, ln1_g, ln1_b, ln2_g, ln2_b, w_router, b_router, w_gu, b_gu, w_down, b_down):
    Bp, Lp, D = x_prompt.shape
    Bs, Ls, _ = x_sample.shape
    depth = w_in.shape[0]
    d_a = w0.shape[1]
    rw_heads, rw_head = r_k.shape[1], r_k.shape[2]
    hg_dv = hg_norm_w.shape[1]
    nk = hg_lower.shape[1]
    d_b = D - d_a
    hg_heads = d_b // hg_dv
    hg_dk = nk // hg_heads
    nv = hg_heads * hg_dv
    n_exp = w_router.shape[-1]
    d_ff = w_down.shape[2]
    assert Lp % GROUP == 0 and Ls % GROUP == 0
    Tp, Ts = Bp * Lp, Bs * Ls
    T = Tp + Ts
    dn_alpha = (2 * depth) ** 0.25

    def split_groups(z):
        return z[:Tp].reshape(Bp, Lp, -1), z[Tp:].reshape(Bs, Ls, -1)

    def shift_tokens(z, prev_s):
        zp, zs = split_groups(z)
        zp = jnp.concatenate([jnp.zeros_like(zp[:, :1]), zp[:, :-1]], axis=1)
        zs = jnp.concatenate([prev_s[:, None], zs[:, :-1]], axis=1)
        return jnp.concatenate([zp.reshape(Tp, -1), zs.reshape(Ts, -1)])

    c_all = jax.nn.silu(jnp.concatenate([c_prompt, c_sample]))
    n_c = Bp + Bs
    c_pad = jnp.pad(c_all, ((0, (-n_c) % SUBLANES), (0, 0)))
    b_ada3 = b_ada[:, None, :]
    rep = jnp.concatenate([jnp.repeat(jnp.arange(Bp), Lp // GROUP), Bp + jnp.repeat(jnp.arange(Bs), Ls // GROUP)])

    p_lb = jax.nn.softmax(hg_lower.astype(F32), axis=0)
    lb_all = jnp.cumsum(p_lb, axis=0) - p_lb[0]

    x = jnp.concatenate([x_prompt.reshape(Tp, D), x_sample.reshape(Ts, D)])
    ln1_g3, ln1_b3, ln2_g3, ln2_b3 = (t[:, None, :] for t in (ln1_g, ln1_b, ln2_g, ln2_b))
    b_router3 = b_router[:, None, :]
    b_gu4 = b_gu[:, :, None, :]
    b_down4 = b_down[:, :, None, :]

    v_first = None
    rw_p, rw_s, hg_p, hg_s, sh_p, sh_s = [], [], [], [], [], []
    for l in range(depth):
        mod_c = mm(c_pad, w_ada, l, b_ada3, name="ada")[:n_c]
        mod = mod_c[rep][:, None, :]
        SH1, SC1, GT1, SH2, SC2, GT2 = range(6)

        xm = modulate(x, mod, SC1, SH1)
        prev = state_shift[l]
        prev_bf = prev.astype(BF16)

        def last_rows(xg, m):
            sh1, sc1 = m[:, SH1 * D:(SH1 + 1) * D], m[:, SC1 * D:(SC1 + 1) * D]
            return xg[:, -1] * (1 + sc1) + sh1
        xp3, xs3 = split_groups(x)
        sh_p.append(last_rows(xp3, mod_c[:Bp]))
        sh_s.append(last_rows(xs3, mod_c[Bp:]))

        z = mm(xm, w_in, l, name="w_in")
        z_prev = mm(prev_bf, w_in, l, n_cols=3 * d_a, name="w_in_prev")

        lora_w = [w1[l], a1[l], g1[l]] + ([v1[l - 1]] if l > 0 else [])
        lora_mu = [mu_in[l, 0], mu_in[l, 1], mu_in[l, 2]] + ([mu_vg[l - 1]] if l > 0 else [])
        wl = jnp.concatenate(lora_w + [m[:, None] * w for m, w in zip(lora_mu, lora_w)], axis=1)
        nl = wl.shape[1] // 2
        zl = mm(xm, wl, name="lora_in")
        zl_prev = mm(prev_bf, wl, name="lora_in_prev")
        lora = zl[:, :nl] + shift_tokens(zl[:, nl:], zl_prev[:, nl:]) - zl[:, nl:]
        offs = [0]
        for w in lora_w:
            offs.append(offs[-1] + w.shape[1])
        lw, la, lg = (lora[:, offs[i]:offs[i + 1]] for i in range(3))

        zA = z[:, :3 * d_a]
        zA = zA + (shift_tokens(zA, z_prev) - zA) * mu_rkv[l]
        r, k, v = jnp.split(zA, 3, axis=-1)
        w_log = -jax.nn.softplus(-(w0[l] + mm(jnp.tanh(lw), w2, l, name="lora_w"))) - 0.5
        if l == 0:
            v_first = v
        else:
            lv = lora[:, offs[3]:offs[4]]
            v = v + (v_first - v) * jax.nn.sigmoid(v0[l - 1] + mm(lv, v2, l - 1, name="lora_v"))
        a = jax.nn.sigmoid(a0[l] + mm(la, a2, l, name="lora_a"))
        g = mm(jax.nn.sigmoid(lg), g2, l, name="lora_g")
        heads = lambda t: t.reshape(T, rw_heads, rw_head)
        kk = heads(k * k_k[l])
        kk = kk / jnp.maximum(jnp.sqrt(jnp.sum(kk * kk, -1, keepdims=True)), 1e-12)
        k = k * (1 + (a - 1) * k_a[l])
        decay = jnp.exp(-jnp.exp(w_log))
        a_vec = (-kk).reshape(T, d_a)
        b_vec = (kk * heads(a)).reshape(T, d_a)

        y_groups, s_groups = [], []
        for gi, (B, L, s0) in enumerate(((Bp, Lp, None), (Bs, Ls, state_rwkv[l]))):
            pick = lambda t: split_groups(t)[gi].reshape(B, L, rw_heads, rw_head)
            ins = [to_lanes(pick(t), RW_SPLIT) for t in (a_vec, decay, b_vec, k, r)]
            vl = vals_to_lanes(pick(v), RW_SPLIT)
            vr = rw_head // RW_SPLIT
            nlanes = B * rw_heads * RW_SPLIT
            if s0 is None:
                s0l = jnp.zeros((vr, rw_head, nlanes), F32)
            else:
                s0l = jnp.transpose(s0.reshape(B, rw_heads, RW_SPLIT, vr, rw_head), (3, 4, 0, 1, 2)).reshape(vr, rw_head, nlanes)
            yl, sl = rwkv_scan(*ins, vl, s0l)
            y_groups.append(vals_from_lanes(yl, B, rw_heads, RW_SPLIT).reshape(B * L, d_a))
            s_groups.append(jnp.transpose(sl.reshape(vr, rw_head, B, rw_heads, RW_SPLIT), (2, 3, 4, 0, 1)).reshape(B, rw_heads, rw_head, rw_head))
        rw_p.append(s_groups[0])
        rw_s.append(s_groups[1])
        y = heads(jnp.concatenate(y_groups))
        mu_y = jnp.mean(y, -1, keepdims=True)
        var_y = jnp.mean(jnp.square(y - mu_y), -1, keepdims=True)
        y = ((y - mu_y) * lax.rsqrt(var_y + RW_LN_EPS)).reshape(T, d_a) * lnx_g[l] + lnx_b[l]
        bonus = (jnp.sum(heads(r) * heads(k) * r_k[l], -1, keepdims=True) * heads(v)).reshape(T, d_a)
        oA = (y + bonus) * g

        zB = z[:, 3 * d_a:]
        zq, zf, zi, zg = jnp.split(zB, [nk, 2 * nk, 2 * nk + nv], axis=-1)
        lb = lb_all[l]
        q = jax.nn.silu(zq)
        f = jnp.maximum(lb + (1 - lb) * jax.nn.sigmoid(zf), F_FLOOR)
        kf = (1 - lb) * jax.nn.sigmoid(-zf)
        o_groups, s_groups = [], []
        for gi, (B, L, s0) in enumerate(((Bp, Lp, None), (Bs, Ls, state_hgrn[l]))):
            pick = lambda t, c: split_groups(t)[gi].reshape(B, L, hg_heads, c)
            ins = [to_lanes(pick(t, hg_dk), HG_SPLIT) for t in (f, kf, q)]
            vl = vals_to_lanes(pick(zi, hg_dv), HG_SPLIT)
            vc = hg_dv // HG_SPLIT
            nlanes = B * hg_heads * HG_SPLIT
            if s0 is None:
                s0l = jnp.zeros((vc, hg_dk, nlanes), F32)
            else:
                s0l = jnp.transpose(s0.reshape(B, hg_heads, hg_dk, HG_SPLIT, vc), (4, 2, 0, 1, 3)).reshape(vc, hg_dk, nlanes)
            ol, sl = hgrn_scan(*ins, vl, s0l)
            o_groups.append(vals_from_lanes(ol, B, hg_heads, HG_SPLIT).reshape(B * L, nv))
            s_groups.append(jnp.transpose(sl.reshape(vc, hg_dk, B, hg_heads, HG_SPLIT), (2, 3, 1, 4, 0)).reshape(B, hg_heads, hg_dk, hg_dv))
        hg_p.append(s_groups[0])
        hg_s.append(s_groups[1])
        o = jnp.concatenate(o_groups).reshape(T, hg_heads, hg_dv)
        o = o * lax.rsqrt(jnp.mean(o * o, -1, keepdims=True) + LN_EPS) * hg_norm_w[l]
        oB = o.reshape(T, d_b) * jax.nn.silu(zg)

        mix = mm(jnp.concatenate([oA, oB], axis=-1).astype(BF16), w_out, l, name="w_out")
        h, hm, logits = ln_router(x, mix, mod, GT1, SC2, SH2, ln1_g3, ln1_b3, w_router, b_router3, l, dn_alpha)
        ffn = moe(hm, logits, w_gu, b_gu4, w_down, b_down4, l)
        x = ln(h, ffn, mod, GT2, ln2_g3, ln2_b3, l, dn_alpha)

    y_prompt = x[:Tp].reshape(Bp, Lp, D)
    y_sample = x[Tp:].reshape(Bs, Ls, D)
    return (y_prompt, y_sample, jnp.stack(rw_p), jnp.stack(rw_s), jnp.stack(hg_p), jnp.stack(hg_s),
            jnp.stack(sh_p), jnp.stack(sh_s))
```

```python
import functools

import jax
import jax.numpy as jnp
from jax import lax
from jax.experimental import pallas as pl
from jax.experimental.pallas import tpu as pltpu

F32 = jnp.float32
BF16 = jnp.bfloat16

TOP_K = 4
SWIGLU_LIMIT = 7.0
SWIGLU_ALPHA = 1.702
LN_EPS = 1e-5
RW_LN_EPS = 64e-5
F_FLOOR = 1e-30

LANES = 128
SUBLANES = 8
GROUP = SUBLANES
VMEM_LIMIT = 56 * 1024 * 1024

MM_TM = 1024
MM_TN = 1024
EW_GROUPS = 64
MOE_TM = 512
MOE_TF = 512
COMBINE_TT = 64
RW_SPLIT = 2
HG_CHUNK = 32
SEQ_TB = 32


def _divisor(n, pref):
    d = min(n, pref)
    while n % d:
        d -= 1
    return d


def _params(sem):
    return pltpu.CompilerParams(dimension_semantics=sem, vmem_limit_bytes=VMEM_LIMIT)


def _mm_kernel(x_ref, w_ref, o_ref, wbf_ref):
    @pl.when(pl.program_id(1) == 0)
    def _():
        wbf_ref[...] = w_ref[...].astype(BF16)

    o_ref[...] = jnp.dot(x_ref[...].astype(BF16), wbf_ref[...], preferred_element_type=F32)


def _mm_bias_kernel(x_ref, w_ref, b_ref, o_ref, wbf_ref):
    @pl.when(pl.program_id(1) == 0)
    def _():
        wbf_ref[...] = w_ref[...].astype(BF16)

    o_ref[...] = jnp.dot(x_ref[...].astype(BF16), wbf_ref[...], preferred_element_type=F32) + b_ref[...]


def mm(x, w, layer=None, bias=None, *, n_cols=None, name="mm"):
    M, K = x.shape
    N = w.shape[-1] if n_cols is None else n_cols
    tm = MM_TM if M % MM_TM == 0 else M
    tn = MM_TN if N % MM_TN == 0 else N
    grid = (N // tn, M // tm)
    if layer is None:
        w_spec = pl.BlockSpec((K, tn), lambda j, i: (0, j))
    else:
        w_spec = pl.BlockSpec((None, K, tn), lambda j, i: (layer, 0, j))
    in_specs = [pl.BlockSpec((tm, K), lambda j, i: (i, 0)), w_spec]
    args = [x, w]
    kern = _mm_kernel
    if bias is not None:
        if layer is None:
            in_specs.append(pl.BlockSpec((1, tn), lambda j, i: (0, j)))
        else:
            in_specs.append(pl.BlockSpec((None, 1, tn), lambda j, i: (layer, 0, j)))
        args.append(bias)
        kern = _mm_bias_kernel
    return pl.pallas_call(
        kern,
        grid=grid,
        in_specs=in_specs,
        out_specs=pl.BlockSpec((tm, tn), lambda j, i: (i, j)),
        out_shape=jax.ShapeDtypeStruct((M, N), F32),
        scratch_shapes=[pltpu.VMEM((K, tn), BF16)],
        compiler_params=_params(("arbitrary", "arbitrary")),
        name=name,
    )(*args)


def _modulate_kernel(x_ref, sc_ref, sh_ref, o_ref):
    gb = sc_ref.shape[0]
    d = x_ref.shape[-1]
    x = x_ref[...].reshape(gb, GROUP, d)
    xm = x * (1.0 + sc_ref[...]) + sh_ref[...]
    o_ref[...] = xm.reshape(gb * GROUP, d).astype(o_ref.dtype)


def modulate(x, mod, sc_idx, sh_idx):
    T, D = x.shape
    G = T // GROUP
    gb = _divisor(G, EW_GROUPS)
    return pl.pallas_call(
        _modulate_kernel,
        grid=(G // gb,),
        in_specs=[
            pl.BlockSpec((gb * GROUP, D), lambda i: (i, 0)),
            pl.BlockSpec((gb, 1, D), lambda i: (i, 0, sc_idx)),
            pl.BlockSpec((gb, 1, D), lambda i: (i, 0, sh_idx)),
        ],
        out_specs=pl.BlockSpec((gb * GROUP, D), lambda i: (i, 0)),
        out_shape=jax.ShapeDtypeStruct((T, D), BF16),
        compiler_params=_params(("arbitrary",)),
        name="modulate",
    )(x, mod, mod)


def _deepnorm(x_ref, y_ref, gt_ref, g_ref, b_ref, alpha):
    gb = gt_ref.shape[0]
    d = x_ref.shape[-1]
    x = x_ref[...].reshape(gb, GROUP, d)
    y = y_ref[...].reshape(gb, GROUP, d)
    u = alpha * x + (1.0 + gt_ref[...]) * y
    mu = jnp.mean(u, axis=-1, keepdims=True)
    uc = u - mu
    var = jnp.mean(uc * uc, axis=-1, keepdims=True)
    return uc * lax.rsqrt(var + LN_EPS) * g_ref[...] + b_ref[...]


def _ln_router_kernel(x_ref, y_ref, gt_ref, g_ref, b_ref, sc_ref, sh_ref, wr_ref, br_ref,
                      h_ref, hm_ref, lg_ref, *, alpha):
    gb = gt_ref.shape[0]
    d = x_ref.shape[-1]
    h = _deepnorm(x_ref, y_ref, gt_ref, g_ref, b_ref, alpha)
    hm = (h * (1.0 + sc_ref[...]) + sh_ref[...]).reshape(gb * GROUP, d)
    h_ref[...] = h.reshape(gb * GROUP, d)
    hm_ref[...] = hm.astype(BF16)
    lg_ref[...] = jnp.dot(hm, wr_ref[...], preferred_element_type=F32,
                          precision=lax.Precision.HIGHEST) + br_ref[...]


def ln_router(x, y, mod, gt_idx, sc_idx, sh_idx, g, b, w_router, b_router, layer, alpha):
    T, D = x.shape
    G = T // GROUP
    gb = _divisor(G, EW_GROUPS)
    E = w_router.shape[-1]
    rows = pl.BlockSpec((gb * GROUP, D), lambda i: (i, 0))
    vec = pl.BlockSpec((None, 1, D), lambda i: (layer, 0, 0))
    return pl.pallas_call(
        functools.partial(_ln_router_kernel, alpha=alpha),
        grid=(G // gb,),
        in_specs=[
            rows, rows,
            pl.BlockSpec((gb, 1, D), lambda i: (i, 0, gt_idx)),
            vec, vec,
            pl.BlockSpec((gb, 1, D), lambda i: (i, 0, sc_idx)),
            pl.BlockSpec((gb, 1, D), lambda i: (i, 0, sh_idx)),
            pl.BlockSpec((None, D, E), lambda i: (layer, 0, 0)),
            pl.BlockSpec((None, 1, E), lambda i: (layer, 0, 0)),
        ],
        out_specs=[rows, rows, pl.BlockSpec((gb * GROUP, E), lambda i: (i, 0))],
        out_shape=[
            jax.ShapeDtypeStruct((T, D), F32),
            jax.ShapeDtypeStruct((T, D), BF16),
            jax.ShapeDtypeStruct((T, E), F32),
        ],
        compiler_params=_params(("arbitrary",)),
        name="ln_router",
    )(x, y, mod, g, b, mod, mod, w_router, b_router)


def _combine_ln_kernel(pos_ref, h_ref, yb_ref, gt_ref, g_ref, b_ref, o_ref, buf_ref, sem_ref, *, alpha):
    i = pl.program_id(0)
    n = pl.num_programs(0)
    tt = h_ref.shape[0]

    def row_copies(tile, slot, act):
        for j in range(TOP_K):
            def body(tok, c):
                p = pos_ref[(tile * tt + tok) * TOP_K + j]
                act(pltpu.make_async_copy(yb_ref.at[pl.ds(p, 1)],
                                          buf_ref.at[slot, pl.ds(j * tt + tok, 1)],
                                          sem_ref.at[slot]))
                return c
            lax.fori_loop(0, tt, body, 0)

    @pl.when(i == 0)
    def _():
        row_copies(0, 0, lambda cp: cp.start())

    @pl.when(i + 1 < n)
    def _():
        row_copies(i + 1, (i + 1) % 2, lambda cp: cp.start())

    slot = i % 2
    row_copies(i, slot, lambda cp: cp.wait())
    y = buf_ref[slot, pl.ds(0, tt), :]
    for j in range(1, TOP_K):
        y = y + buf_ref[slot, pl.ds(j * tt, tt), :]
    gb = gt_ref.shape[0]
    d = h_ref.shape[-1]
    u = alpha * h_ref[...].reshape(gb, GROUP, d) + (1.0 + gt_ref[...]) * y.reshape(gb, GROUP, d)
    mu = jnp.mean(u, axis=-1, keepdims=True)
    uc = u - mu
    var = jnp.mean(uc * uc, axis=-1, keepdims=True)
    o_ref[...] = (uc * lax.rsqrt(var + LN_EPS) * g_ref[...] + b_ref[...]).reshape(gb * GROUP, d)


def combine_ln(h, yb, pos, mod, gt_idx, g, b, layer, alpha):
    T, D = h.shape
    tt = COMBINE_TT
    gb = tt // GROUP
    assert T % tt == 0
    grid_spec = pltpu.PrefetchScalarGridSpec(
        num_scalar_prefetch=1,
        grid=(T // tt,),
        in_specs=[
            pl.BlockSpec((tt, D), lambda i, pos: (i, 0)),
            pl.BlockSpec(memory_space=pl.ANY),
            pl.BlockSpec((gb, 1, D), lambda i, pos: (i, 0, gt_idx)),
            pl.BlockSpec((None, 1, D), lambda i, pos: (layer, 0, 0)),
            pl.BlockSpec((None, 1, D), lambda i, pos: (layer, 0, 0)),
        ],
        out_specs=pl.BlockSpec((tt, D), lambda i, pos: (i, 0)),
        scratch_shapes=[pltpu.VMEM((2, TOP_K * tt, D), F32), pltpu.SemaphoreType.DMA((2,))],
    )
    return pl.pallas_call(
        functools.partial(_combine_ln_kernel, alpha=alpha),
        grid_spec=grid_spec,
        out_shape=jax.ShapeDtypeStruct((T, D), F32),
        compiler_params=_params(("arbitrary",)),
        name="combine_ln",
    )(pos, h, yb, mod, g, b)


def _rwkv_kernel(a_ref, d_ref, b_ref, k_ref, r_ref, v_ref, s0_ref, y_ref, s_ref):
    @pl.when(pl.program_id(1) == 0)
    def _():
        s_ref[...] = s0_ref[...]

    n_steps = a_ref.shape[0]
    n_rows = v_ref.shape[1]

    def step(t, carry):
        a = a_ref[t]
        d = d_ref[t]
        b = b_ref[t]
        k = k_ref[t]
        r = r_ref[t]

        def row(v, c):
            s = s_ref[v]
            sa = jnp.sum(s * a, axis=0, keepdims=True)
            vv = v_ref[t, pl.ds(v, 1), :]
            s = s * d + sa * b + vv * k
            s_ref[v] = s
            y_ref[t, pl.ds(v, 1), :] = jnp.sum(s * r, axis=0, keepdims=True)
            return c

        return lax.fori_loop(0, n_rows, row, carry, unroll=True)

    lax.fori_loop(0, n_steps, step, 0)


def rwkv_scan(a, d, b, k, r, v, s0):
    L, K, NL = a.shape
    VR = v.shape[1]
    tb = SEQ_TB if L % SEQ_TB == 0 else L
    vec = pl.BlockSpec((tb, K, LANES), lambda g, t: (t, 0, g))
    val = pl.BlockSpec((tb, VR, LANES), lambda g, t: (t, 0, g))
    st = pl.BlockSpec((VR, K, LANES), lambda g, t: (0, 0, g))
    return pl.pallas_call(
        _rwkv_kernel,
        grid=(NL // LANES, L // tb),
        in_specs=[vec, vec, vec, vec, vec, val, st],
        out_specs=[val, st],
        out_shape=[jax.ShapeDtypeStruct((L, VR, NL), F32), jax.ShapeDtypeStruct((VR, K, NL), F32)],
        compiler_params=_params(("arbitrary", "arbitrary")),
        name="rwkv_scan",
    )(a, d, b, k, r, v, s0)


def _hgrn_kernel(*refs, n_heads, has_state):
    if has_state:
        zq_ref, zf_ref, zi_ref, zg_ref, lb_ref, nw_ref, s0_ref, o_ref, s_ref, st_ref, oi_ref = refs
    else:
        zq_ref, zf_ref, zi_ref, zg_ref, lb_ref, nw_ref, o_ref, s_ref, st_ref, oi_ref = refs
    c = pl.program_id(1)
    last = pl.num_programs(1) - 1
    C = zq_ref.shape[0]
    dk = st_ref.shape[2]

    @pl.when(c == 0)
    def _():
        if has_state:
            for h in range(n_heads):
                st_ref[h] = s0_ref[h].T
        else:
            st_ref[...] = jnp.zeros_like(st_ref)

    row = lax.broadcasted_iota(jnp.int32, (C, C), 0)
    col = lax.broadcasted_iota(jnp.int32, (C, C), 1)
    tril = (row >= col).astype(F32)
    for h in range(n_heads):
        sl = slice(h * dk, (h + 1) * dk)
        zq, zf, v, zg = zq_ref[:, sl], zf_ref[:, sl], zi_ref[:, sl], zg_ref[:, sl]
        lb = lb_ref[:, sl]
        q = zq * jax.nn.sigmoid(zq)
        f = lb + (1.0 - lb) * jax.nn.sigmoid(zf)
        kf = (1.0 - lb) * jax.nn.sigmoid(-zf)
        g = jnp.log(jnp.maximum(f, F_FLOOR))
        b = jnp.dot(tril, g, preferred_element_type=F32, precision=lax.Precision.HIGHEST)
        st = st_ref[h]
        o = lax.dot_general(q * jnp.exp(b), st, (((1,), (1,)), ((), ())), preferred_element_type=F32)
        for t in range(C):
            n = min(C, -(-(t + 1) // SUBLANES) * SUBLANES)
            keep = lax.broadcasted_iota(jnp.int32, (n, 1), 0) <= t
            diff = b[t:t + 1, :] - b[:n]
            e = jnp.where(keep, jnp.exp(jnp.where(keep, diff, 0.0)), 0.0)
            a_t = jnp.sum(q[t:t + 1, :] * kf[:n] * e, axis=-1, keepdims=True)
            oi_ref[pl.ds(t, 1), :] = jnp.sum(a_t * v[:n], axis=0, keepdims=True)
        o = o + oi_ref[...]
        b_last = b[C - 1:C, :]
        kdec = kf * jnp.exp(b_last - b)
        st_ref[h] = st * jnp.exp(b_last) + jnp.dot(v.T, kdec, preferred_element_type=F32,
                                                   precision=lax.Precision.HIGHEST)
        o = o * lax.rsqrt(jnp.mean(o * o, axis=-1, keepdims=True) + LN_EPS) * nw_ref[...]
        o_ref[:, sl] = (o * (zg * jax.nn.sigmoid(zg))).astype(o_ref.dtype)

    @pl.when(c == last)
    def _():
        for h in range(n_heads):
            s_ref[h] = st_ref[h].T


def hgrn_mix(z, col0, lb, nw, layer, s0, B, L, row0, n_heads, dk, dv):
    assert dk == dv and dk % LANES == 0
    width = n_heads * dk
    C = _divisor(L, HG_CHUNK)
    nc = L // C
    assert col0 % width == 0 and row0 % C == 0
    cb, rb = col0 // width, row0 // C

    def zspec(k):
        return pl.BlockSpec((C, width), lambda b, c: (rb + b * nc + c, cb + k))

    in_specs = [zspec(0), zspec(1), zspec(2), zspec(3),
                pl.BlockSpec((None, 1, width), lambda b, c: (layer, 0, 0)),
                pl.BlockSpec((None, 1, dv), lambda b, c: (layer, 0, 0))]
    args = [z, z, z, z, lb, nw]
    st_spec = pl.BlockSpec((None, n_heads, dk, dv), lambda b, c: (b, 0, 0, 0))
    if s0 is not None:
        in_specs.append(st_spec)
        args.append(s0)
    return pl.pallas_call(
        functools.partial(_hgrn_kernel, n_heads=n_heads, has_state=s0 is not None),
        grid=(B, nc),
        in_specs=in_specs,
        out_specs=[pl.BlockSpec((C, width), lambda b, c: (b * nc + c, 0)), st_spec],
        out_shape=[jax.ShapeDtypeStruct((B * L, width), F32),
                   jax.ShapeDtypeStruct((B, n_heads, dk, dv), F32)],
        scratch_shapes=[pltpu.VMEM((n_heads, dv, dk), F32), pltpu.VMEM((C, dv), F32)],
        compiler_params=_params(("arbitrary", "arbitrary")),
        name="hgrn_mix",
    )(*args)


def to_lanes(x, split):
    B, L, H, C = x.shape
    xt = jnp.transpose(x, (1, 3, 0, 2))[..., None]
    return jnp.broadcast_to(xt, (L, C, B, H, split)).reshape(L, C, B * H * split)


def vals_to_lanes(v, split):
    B, L, H, C = v.shape
    vt = jnp.transpose(v.reshape(B, L, H, split, C // split), (1, 4, 0, 2, 3))
    return vt.reshape(L, C // split, B * H * split)


def vals_from_lanes(y, B, H, split):
    L, cs, _ = y.shape
    yt = jnp.transpose(y.reshape(L, cs, B, H, split), (2, 0, 3, 4, 1))
    return yt.reshape(B, L, H * split * cs)


def _moe_kernel(be_ref, nu_ref, x_ref, wp_ref, wg_ref, wl_ref, bg_ref, bl_ref, wd_ref, bd_ref,
                o_ref, acc_ref):
    i = pl.program_id(0)
    j = pl.program_id(1)
    last = pl.num_programs(1) - 1
    active = i < nu_ref[0]

    @pl.when(active)
    def _():
        x = x_ref[...]
        g = jnp.dot(x, wg_ref[...].astype(BF16), preferred_element_type=F32) + bg_ref[...]
        lin = jnp.dot(x, wl_ref[...].astype(BF16), preferred_element_type=F32) + bl_ref[...]
        glu = jnp.minimum(g, SWIGLU_LIMIT)
        lin = jnp.clip(lin, -SWIGLU_LIMIT, SWIGLU_LIMIT)
        act = (lin + 1.0) * glu * jax.nn.sigmoid(SWIGLU_ALPHA * glu)
        part = jnp.dot(act.astype(BF16), wd_ref[...].astype(BF16), preferred_element_type=F32)

        @pl.when(j == 0)
        def _():
            acc_ref[...] = part

        @pl.when(j > 0)
        def _():
            acc_ref[...] += part

        @pl.when(j == last)
        def _():
            o_ref[...] = (acc_ref[...] + bd_ref[...]) * wp_ref[...]

    @pl.when(jnp.logical_and(jnp.logical_not(active), j == last))
    def _():
        o_ref[...] = jnp.zeros_like(o_ref)


def moe_experts(block_e, n_used, xb, w_pad, w_gu, b_gu, w_down, b_down, layer):
    rows, D = xb.shape
    d_ff = w_down.shape[2]
    tm, tf = MOE_TM, MOE_TF
    nff = d_ff // tf
    n_blocks = rows // tm

    def hold(i, j, nu):
        return jnp.where(i < nu[0], j, nff - 1)

    def row_blk(i, nu):
        return jnp.minimum(i, jnp.maximum(nu[0] - 1, 0))

    grid_spec = pltpu.PrefetchScalarGridSpec(
        num_scalar_prefetch=2,
        grid=(n_blocks, nff),
        in_specs=[
            pl.BlockSpec((tm, D), lambda i, j, be, nu: (row_blk(i, nu), 0)),
            pl.BlockSpec((tm, 1), lambda i, j, be, nu: (row_blk(i, nu), 0)),
            pl.BlockSpec((None, None, D, tf), lambda i, j, be, nu: (layer, be[i], 0, hold(i, j, nu))),
            pl.BlockSpec((None, None, D, tf), lambda i, j, be, nu: (layer, be[i], 0, hold(i, j, nu) + nff)),
            pl.BlockSpec((None, None, 1, tf), lambda i, j, be, nu: (layer, be[i], 0, hold(i, j, nu))),
            pl.BlockSpec((None, None, 1, tf), lambda i, j, be, nu: (layer, be[i], 0, hold(i, j, nu) + nff)),
            pl.BlockSpec((None, None, tf, D), lambda i, j, be, nu: (layer, be[i], hold(i, j, nu), 0)),
            pl.BlockSpec((None, None, 1, D), lambda i, j, be, nu: (layer, be[i], 0, 0)),
        ],
        out_specs=pl.BlockSpec((tm, D), lambda i, j, be, nu: (i, 0)),
        scratch_shapes=[pltpu.VMEM((tm, D), F32)],
    )
    return pl.pallas_call(
        _moe_kernel,
        grid_spec=grid_spec,
        out_shape=jax.ShapeDtypeStruct((rows, D), F32),
        compiler_params=_params(("arbitrary", "arbitrary")),
        name="moe_experts",
    )(block_e, n_used, xb, w_pad, w_gu, w_gu, b_gu, b_gu, w_down, b_down)


def moe(hm, logits, w_gu, b_gu, w_down, b_down, layer):
    T, D = hm.shape
    E = logits.shape[-1]
    blk = MOE_TM
    top_val, top_idx = lax.top_k(logits, TOP_K)
    gates = jax.nn.softmax(top_val, axis=-1)
    TK = T * TOP_K
    flat_e = top_idx.reshape(TK).astype(jnp.int32)
    order = jnp.argsort(flat_e)
    e_sorted = flat_e[order]
    tok_sorted = (order // TOP_K).astype(jnp.int32)
    w_sorted = gates.reshape(TK)[order]
    counts = jnp.bincount(flat_e, length=E).astype(jnp.int32)
    padded = (counts + blk - 1) // blk * blk
    pad_end = jnp.cumsum(padded)
    dest = (pad_end - padded)[e_sorted] + jnp.arange(TK, dtype=jnp.int32) - (jnp.cumsum(counts) - counts)[e_sorted]
    n_blocks = -(-(TK + E * (blk - 1)) // blk)
    rows = n_blocks * blk
    tok_pad = jnp.full((rows,), T, jnp.int32).at[dest].set(tok_sorted)
    w_pad = jnp.zeros((rows,), F32).at[dest].set(w_sorted)
    block_e = jnp.minimum(jnp.searchsorted(pad_end, jnp.arange(n_blocks, dtype=jnp.int32) * blk, side='right'),
                          E - 1).astype(jnp.int32)
    n_used = (pad_end[-1] // blk).astype(jnp.int32).reshape(1)
    xb = jnp.concatenate([hm, jnp.zeros((1, D), hm.dtype)])[tok_pad]
    yb = moe_experts(block_e, n_used, xb, w_pad[:, None], w_gu, b_gu, w_down, b_down, layer)
    pos = jnp.zeros((TK,), jnp.int32).at[order].set(dest)
    return yb, pos


def kernel(x_prompt, x_sample, c_prompt, c_sample, state_rwkv, state_hgrn, state_shift, w_ada, b_ada, w_in, w_out, mu_rkv, mu_in, w0, w1, w2, a0, a1, a2, v0, v1, v2, mu_vg, g1, g2, k_k, k_a, r_k, lnx_g, lnx_b, hg_lower, hg_norm_w, ln1_g, ln1_b, ln2_g, ln2_b, w_router, b_router, w_gu, b_gu, w_down, b_down):
    Bp, Lp, D = x_prompt.shape
    Bs, Ls, _ = x_sample.shape
    depth = w_in.shape[0]
    d_a = w0.shape[1]
    rw_heads, rw_head = r_k.shape[1], r_k.shape[2]
    hg_dv = hg_norm_w.shape[1]
    nk = hg_lower.shape[1]
    d_b = D - d_a
    hg_heads = d_b // hg_dv
    hg_dk = nk // hg_heads
    nv = hg_heads * hg_dv
    n_exp = w_router.shape[-1]
    d_ff = w_down.shape[2]
    assert Lp % GROUP == 0 and Ls % GROUP == 0
    Tp, Ts = Bp * Lp, Bs * Ls
    T = Tp + Ts
    dn_alpha = (2 * depth) ** 0.25

    def split_groups(z):
        return z[:Tp].reshape(Bp, Lp, -1), z[Tp:].reshape(Bs, Ls, -1)

    def shift_tokens(z, prev_s):
        zp, zs = split_groups(z)
        zp = jnp.concatenate([jnp.zeros_like(zp[:, :1]), zp[:, :-1]], axis=1)
        zs = jnp.concatenate([prev_s[:, None], zs[:, :-1]], axis=1)
        return jnp.concatenate([zp.reshape(Tp, -1), zs.reshape(Ts, -1)])

    c_all = jax.nn.silu(jnp.concatenate([c_prompt, c_sample]))
    n_c = Bp + Bs
    c_pad = jnp.pad(c_all, ((0, (-n_c) % SUBLANES), (0, 0)))
    b_ada3 = b_ada[:, None, :]
    rep = jnp.concatenate([jnp.repeat(jnp.arange(Bp), Lp // GROUP), Bp + jnp.repeat(jnp.arange(Bs), Ls // GROUP)])

    p_lb = jax.nn.softmax(hg_lower.astype(F32), axis=0)
    lb_all3 = (jnp.cumsum(p_lb, axis=0) - p_lb[0])[:, None, :]
    hg_norm_w3 = hg_norm_w[:, None, :]
    assert nk == nv

    x = jnp.concatenate([x_prompt.reshape(Tp, D), x_sample.reshape(Ts, D)])
    ln1_g3, ln1_b3, ln2_g3, ln2_b3 = (t[:, None, :] for t in (ln1_g, ln1_b, ln2_g, ln2_b))
    b_router3 = b_router[:, None, :]
    b_gu4 = b_gu[:, :, None, :]
    b_down4 = b_down[:, :, None, :]

    v_first = None
    rw_p, rw_s, hg_p, hg_s, sh_p, sh_s = [], [], [], [], [], []
    for l in range(depth):
        mod_c = mm(c_pad, w_ada, l, b_ada3, name="ada")[:n_c]
        mod = mod_c[rep][:, None, :]
        SH1, SC1, GT1, SH2, SC2, GT2 = range(6)

        xm = modulate(x, mod, SC1, SH1)
        prev = state_shift[l]
        prev_bf = prev.astype(BF16)

        def last_rows(xg, m):
            sh1, sc1 = m[:, SH1 * D:(SH1 + 1) * D], m[:, SC1 * D:(SC1 + 1) * D]
            return xg[:, -1] * (1 + sc1) + sh1
        xp3, xs3 = split_groups(x)
        sh_p.append(last_rows(xp3, mod_c[:Bp]))
        sh_s.append(last_rows(xs3, mod_c[Bp:]))

        z = mm(xm, w_in, l, name="w_in")
        z_prev = mm(prev_bf, w_in, l, n_cols=3 * d_a, name="w_in_prev")

        lora_w = [w1[l], a1[l], g1[l]] + ([v1[l - 1]] if l > 0 else [])
        lora_mu = [mu_in[l, 0], mu_in[l, 1], mu_in[l, 2]] + ([mu_vg[l - 1]] if l > 0 else [])
        wl = jnp.concatenate(lora_w + [m[:, None] * w for m, w in zip(lora_mu, lora_w)], axis=1)
        nl = wl.shape[1] // 2
        zl = mm(xm, wl, name="lora_in")
        zl_prev = mm(prev_bf, wl, name="lora_in_prev")
        lora = zl[:, :nl] + shift_tokens(zl[:, nl:], zl_prev[:, nl:]) - zl[:, nl:]
        offs = [0]
        for w in lora_w:
            offs.append(offs[-1] + w.shape[1])
        lw, la, lg = (lora[:, offs[i]:offs[i + 1]] for i in range(3))

        zA = z[:, :3 * d_a]
        zA = zA + (shift_tokens(zA, z_prev) - zA) * mu_rkv[l]
        r, k, v = jnp.split(zA, 3, axis=-1)
        w_log = -jax.nn.softplus(-(w0[l] + mm(jnp.tanh(lw), w2, l, name="lora_w"))) - 0.5
        if l == 0:
            v_first = v
        else:
            lv = lora[:, offs[3]:offs[4]]
            v = v + (v_first - v) * jax.nn.sigmoid(v0[l - 1] + mm(lv, v2, l - 1, name="lora_v"))
        a = jax.nn.sigmoid(a0[l] + mm(la, a2, l, name="lora_a"))
        g = mm(jax.nn.sigmoid(lg), g2, l, name="lora_g")
        heads = lambda t: t.reshape(T, rw_heads, rw_head)
        kk = heads(k * k_k[l])
        kk = kk / jnp.maximum(jnp.sqrt(jnp.sum(kk * kk, -1, keepdims=True)), 1e-12)
        k = k * (1 + (a - 1) * k_a[l])
        decay = jnp.exp(-jnp.exp(w_log))
        a_vec = (-kk).reshape(T, d_a)
        b_vec = (kk * heads(a)).reshape(T, d_a)

        y_groups, s_groups = [], []
        for gi, (B, L, s0) in enumerate(((Bp, Lp, None), (Bs, Ls, state_rwkv[l]))):
            pick = lambda t: split_groups(t)[gi].reshape(B, L, rw_heads, rw_head)
            ins = [to_lanes(pick(t), RW_SPLIT) for t in (a_vec, decay, b_vec, k, r)]
            vl = vals_to_lanes(pick(v), RW_SPLIT)
            vr = rw_head // RW_SPLIT
            nlanes = B * rw_heads * RW_SPLIT
            if s0 is None:
                s0l = jnp.zeros((vr, rw_head, nlanes), F32)
            else:
                s0l = jnp.transpose(s0.reshape(B, rw_heads, RW_SPLIT, vr, rw_head), (3, 4, 0, 1, 2)).reshape(vr, rw_head, nlanes)
            yl, sl = rwkv_scan(*ins, vl, s0l)
            y_groups.append(vals_from_lanes(yl, B, rw_heads, RW_SPLIT).reshape(B * L, d_a))
            s_groups.append(jnp.transpose(sl.reshape(vr, rw_head, B, rw_heads, RW_SPLIT), (2, 3, 4, 0, 1)).reshape(B, rw_heads, rw_head, rw_head))
        rw_p.append(s_groups[0])
        rw_s.append(s_groups[1])
        y = heads(jnp.concatenate(y_groups))
        mu_y = jnp.mean(y, -1, keepdims=True)
        var_y = jnp.mean(jnp.square(y - mu_y), -1, keepdims=True)
        y = ((y - mu_y) * lax.rsqrt(var_y + RW_LN_EPS)).reshape(T, d_a) * lnx_g[l] + lnx_b[l]
        bonus = (jnp.sum(heads(r) * heads(k) * r_k[l], -1, keepdims=True) * heads(v)).reshape(T, d_a)
        oA = (y + bonus) * g

        oB_p, s_p = hgrn_mix(z, 3 * d_a, lb_all3, hg_norm_w3, l, None, Bp, Lp, 0, hg_heads, hg_dk, hg_dv)
        oB_s, s_s = hgrn_mix(z, 3 * d_a, lb_all3, hg_norm_w3, l, state_hgrn[l], Bs, Ls, Tp, hg_heads, hg_dk, hg_dv)
        hg_p.append(s_p)
        hg_s.append(s_s)
        oB = jnp.concatenate([oB_p, oB_s])

        mix = mm(jnp.concatenate([oA, oB], axis=-1).astype(BF16), w_out, l, name="w_out")
        h, hm, logits = ln_router(x, mix, mod, GT1, SC2, SH2, ln1_g3, ln1_b3, w_router, b_router3, l, dn_alpha)
        yb, pos = moe(hm, logits, w_gu, b_gu4, w_down, b_down4, l)
        x = combine_ln(h, yb, pos, mod, GT2, ln2_g3, ln2_b3, l, dn_alpha)

    y_prompt = x[:Tp].reshape(Bp, Lp, D)
    y_sample = x[Tp:].reshape(Bs, Ls, D)
    return (y_prompt, y_sample, jnp.stack(rw_p), jnp.stack(rw_s), jnp.stack(hg_p), jnp.stack(hg_s),
            jnp.stack(sh_p), jnp.stack(sh_s))
```

```python
import functools

import jax
import jax.numpy as jnp
from jax import lax
from jax.experimental import pallas as pl
from jax.experimental.pallas import tpu as pltpu

F32 = jnp.float32
BF16 = jnp.bfloat16

TOP_K = 4
SWIGLU_LIMIT = 7.0
SWIGLU_ALPHA = 1.702
LN_EPS = 1e-5
RW_LN_EPS = 64e-5
F_FLOOR = 1e-30

LANES = 128
SUBLANES = 8
GROUP = SUBLANES
VMEM_LIMIT = 56 * 1024 * 1024

MM_TM = 1024
MM_TN = 1024
EW_GROUPS = 64
MOE_TM = 512
MOE_TF = 512
COMBINE_TT = 64
DMA_UNROLL = 8
RW_SPLIT = 2
HG_CHUNK = 32
SEQ_TB = 32


def _divisor(n, pref):
    d = min(n, pref)
    while n % d:
        d -= 1
    return d


def _params(sem):
    return pltpu.CompilerParams(dimension_semantics=sem, vmem_limit_bytes=VMEM_LIMIT)


def _mm_kernel(x_ref, w_ref, o_ref, wbf_ref):
    @pl.when(pl.program_id(1) == 0)
    def _():
        wbf_ref[...] = w_ref[...].astype(BF16)

    o_ref[...] = jnp.dot(x_ref[...].astype(BF16), wbf_ref[...], preferred_element_type=F32)


def _mm_bias_kernel(x_ref, w_ref, b_ref, o_ref, wbf_ref):
    @pl.when(pl.program_id(1) == 0)
    def _():
        wbf_ref[...] = w_ref[...].astype(BF16)

    o_ref[...] = jnp.dot(x_ref[...].astype(BF16), wbf_ref[...], preferred_element_type=F32) + b_ref[...]


def mm(x, w, layer=None, bias=None, *, n_cols=None, name="mm"):
    M, K = x.shape
    N = w.shape[-1] if n_cols is None else n_cols
    tm = MM_TM if M % MM_TM == 0 else M
    tn = MM_TN if N % MM_TN == 0 else N
    grid = (N // tn, M // tm)
    if layer is None:
        w_spec = pl.BlockSpec((K, tn), lambda j, i: (0, j))
    else:
        w_spec = pl.BlockSpec((None, K, tn), lambda j, i: (layer, 0, j))
    in_specs = [pl.BlockSpec((tm, K), lambda j, i: (i, 0)), w_spec]
    args = [x, w]
    kern = _mm_kernel
    if bias is not None:
        if layer is None:
            in_specs.append(pl.BlockSpec((1, tn), lambda j, i: (0, j)))
        else:
            in_specs.append(pl.BlockSpec((None, 1, tn), lambda j, i: (layer, 0, j)))
        args.append(bias)
        kern = _mm_bias_kernel
    return pl.pallas_call(
        kern,
        grid=grid,
        in_specs=in_specs,
        out_specs=pl.BlockSpec((tm, tn), lambda j, i: (i, j)),
        out_shape=jax.ShapeDtypeStruct((M, N), F32),
        scratch_shapes=[pltpu.VMEM((K, tn), BF16)],
        compiler_params=_params(("arbitrary", "arbitrary")),
        name=name,
    )(*args)


def _modulate_kernel(x_ref, sc_ref, sh_ref, o_ref):
    gb = sc_ref.shape[0]
    d = x_ref.shape[-1]
    x = x_ref[...].reshape(gb, GROUP, d)
    xm = x * (1.0 + sc_ref[...]) + sh_ref[...]
    o_ref[...] = xm.reshape(gb * GROUP, d).astype(o_ref.dtype)


def modulate(x, mod, sc_idx, sh_idx):
    T, D = x.shape
    G = T // GROUP
    gb = _divisor(G, EW_GROUPS)
    return pl.pallas_call(
        _modulate_kernel,
        grid=(G // gb,),
        in_specs=[
            pl.BlockSpec((gb * GROUP, D), lambda i: (i, 0)),
            pl.BlockSpec((gb, 1, D), lambda i: (i, 0, sc_idx)),
            pl.BlockSpec((gb, 1, D), lambda i: (i, 0, sh_idx)),
        ],
        out_specs=pl.BlockSpec((gb * GROUP, D), lambda i: (i, 0)),
        out_shape=jax.ShapeDtypeStruct((T, D), BF16),
        compiler_params=_params(("arbitrary",)),
        name="modulate",
    )(x, mod, mod)


def _deepnorm(x_ref, y_ref, gt_ref, g_ref, b_ref, alpha):
    gb = gt_ref.shape[0]
    d = x_ref.shape[-1]
    x = x_ref[...].reshape(gb, GROUP, d)
    y = y_ref[...].reshape(gb, GROUP, d)
    u = alpha * x + (1.0 + gt_ref[...]) * y
    mu = jnp.mean(u, axis=-1, keepdims=True)
    uc = u - mu
    var = jnp.mean(uc * uc, axis=-1, keepdims=True)
    return uc * lax.rsqrt(var + LN_EPS) * g_ref[...] + b_ref[...]


def _ln_router_kernel(x_ref, y_ref, gt_ref, g_ref, b_ref, sc_ref, sh_ref, wr_ref, br_ref,
                      h_ref, hm_ref, lg_ref, *, alpha):
    gb = gt_ref.shape[0]
    d = x_ref.shape[-1]
    h = _deepnorm(x_ref, y_ref, gt_ref, g_ref, b_ref, alpha)
    hm = (h * (1.0 + sc_ref[...]) + sh_ref[...]).reshape(gb * GROUP, d)
    h_ref[...] = h.reshape(gb * GROUP, d)
    hm_ref[...] = hm
    lg_ref[...] = jnp.dot(hm, wr_ref[...], preferred_element_type=F32,
                          precision=lax.Precision.HIGHEST) + br_ref[...]


def ln_router(x, y, mod, gt_idx, sc_idx, sh_idx, g, b, w_router, b_router, layer, alpha):
    T, D = x.shape
    G = T // GROUP
    gb = _divisor(G, EW_GROUPS)
    E = w_router.shape[-1]
    rows = pl.BlockSpec((gb * GROUP, D), lambda i: (i, 0))
    vec = pl.BlockSpec((None, 1, D), lambda i: (layer, 0, 0))
    return pl.pallas_call(
        functools.partial(_ln_router_kernel, alpha=alpha),
        grid=(G // gb,),
        in_specs=[
            rows, rows,
            pl.BlockSpec((gb, 1, D), lambda i: (i, 0, gt_idx)),
            vec, vec,
            pl.BlockSpec((gb, 1, D), lambda i: (i, 0, sc_idx)),
            pl.BlockSpec((gb, 1, D), lambda i: (i, 0, sh_idx)),
            pl.BlockSpec((None, D, E), lambda i: (layer, 0, 0)),
            pl.BlockSpec((None, 1, E), lambda i: (layer, 0, 0)),
        ],
        out_specs=[rows, rows, pl.BlockSpec((gb * GROUP, E), lambda i: (i, 0))],
        out_shape=[
            jax.ShapeDtypeStruct((T, D), F32),
            jax.ShapeDtypeStruct((T, D), F32),
            jax.ShapeDtypeStruct((T, E), F32),
        ],
        compiler_params=_params(("arbitrary",)),
        name="ln_router",
    )(x, y, mod, g, b, mod, mod, w_router, b_router)


def _combine_ln_kernel(pos_ref, h_ref, yb_ref, gt_ref, g_ref, b_ref, o_ref, buf_ref, sem_ref, *, alpha):
    i = pl.program_id(0)
    n = pl.num_programs(0)
    tt = h_ref.shape[0]

    def row_copies(tile, slot, act):
        for j in range(TOP_K):
            def body(tok, c):
                p = pos_ref[(tile * tt + tok) * TOP_K + j]
                act(pltpu.make_async_copy(yb_ref.at[pl.ds(p, 1)],
                                          buf_ref.at[slot, pl.ds(j * tt + tok, 1)],
                                          sem_ref.at[slot]))
                return c
            lax.fori_loop(0, tt, body, 0, unroll=DMA_UNROLL)

    @pl.when(i == 0)
    def _():
        row_copies(0, 0, lambda cp: cp.start())

    @pl.when(i + 1 < n)
    def _():
        row_copies(i + 1, (i + 1) % 2, lambda cp: cp.start())

    slot = i % 2
    row_copies(i, slot, lambda cp: cp.wait())
    y = buf_ref[slot, pl.ds(0, tt), :]
    for j in range(1, TOP_K):
        y = y + buf_ref[slot, pl.ds(j * tt, tt), :]
    gb = gt_ref.shape[0]
    d = h_ref.shape[-1]
    u = alpha * h_ref[...].reshape(gb, GROUP, d) + (1.0 + gt_ref[...]) * y.reshape(gb, GROUP, d)
    mu = jnp.mean(u, axis=-1, keepdims=True)
    uc = u - mu
    var = jnp.mean(uc * uc, axis=-1, keepdims=True)
    o_ref[...] = (uc * lax.rsqrt(var + LN_EPS) * g_ref[...] + b_ref[...]).reshape(gb * GROUP, d)


def combine_ln(h, yb, pos, mod, gt_idx, g, b, layer, alpha):
    T, D = h.shape
    tt = COMBINE_TT
    gb = tt // GROUP
    assert T % tt == 0
    grid_spec = pltpu.PrefetchScalarGridSpec(
        num_scalar_prefetch=1,
        grid=(T // tt,),
        in_specs=[
            pl.BlockSpec((tt, D), lambda i, pos: (i, 0)),
            pl.BlockSpec(memory_space=pl.ANY),
            pl.BlockSpec((gb, 1, D), lambda i, pos: (i, 0, gt_idx)),
            pl.BlockSpec((None, 1, D), lambda i, pos: (layer, 0, 0)),
            pl.BlockSpec((None, 1, D), lambda i, pos: (layer, 0, 0)),
        ],
        out_specs=pl.BlockSpec((tt, D), lambda i, pos: (i, 0)),
        scratch_shapes=[pltpu.VMEM((2, TOP_K * tt, D), F32), pltpu.SemaphoreType.DMA((2,))],
    )
    return pl.pallas_call(
        functools.partial(_combine_ln_kernel, alpha=alpha),
        grid_spec=grid_spec,
        out_shape=jax.ShapeDtypeStruct((T, D), F32),
        compiler_params=_params(("arbitrary",)),
        name="combine_ln",
    )(pos, h, yb, mod, g, b)


def _rwkv_kernel(a_ref, d_ref, b_ref, k_ref, r_ref, v_ref, s0_ref, y_ref, s_ref):
    @pl.when(pl.program_id(1) == 0)
    def _():
        s_ref[...] = s0_ref[...]

    n_steps = a_ref.shape[0]
    n_rows = v_ref.shape[1]

    def step(t, carry):
        a = a_ref[t]
        d = d_ref[t]
        b = b_ref[t]
        k = k_ref[t]
        r = r_ref[t]

        def row(v, c):
            s = s_ref[v]
            sa = jnp.sum(s * a, axis=0, keepdims=True)
            vv = v_ref[t, pl.ds(v, 1), :]
            s = s * d + sa * b + vv * k
            s_ref[v] = s
            y_ref[t, pl.ds(v, 1), :] = jnp.sum(s * r, axis=0, keepdims=True)
            return c

        return lax.fori_loop(0, n_rows, row, carry, unroll=True)

    lax.fori_loop(0, n_steps, step, 0)


def rwkv_scan(a, d, b, k, r, v, s0):
    L, K, NL = a.shape
    VR = v.shape[1]
    tb = SEQ_TB if L % SEQ_TB == 0 else L
    vec = pl.BlockSpec((tb, K, LANES), lambda g, t: (t, 0, g))
    val = pl.BlockSpec((tb, VR, LANES), lambda g, t: (t, 0, g))
    st = pl.BlockSpec((VR, K, LANES), lambda g, t: (0, 0, g))
    return pl.pallas_call(
        _rwkv_kernel,
        grid=(NL // LANES, L // tb),
        in_specs=[vec, vec, vec, vec, vec, val, st],
        out_specs=[val, st],
        out_shape=[jax.ShapeDtypeStruct((L, VR, NL), F32), jax.ShapeDtypeStruct((VR, K, NL), F32)],
        compiler_params=_params(("arbitrary", "arbitrary")),
        name="rwkv_scan",
    )(a, d, b, k, r, v, s0)


def _hgrn_kernel(*refs, n_heads, has_state):
    if has_state:
        zq_ref, zf_ref, zi_ref, zg_ref, lb_ref, nw_ref, s0_ref, o_ref, s_ref, st_ref, oi_ref = refs
    else:
        zq_ref, zf_ref, zi_ref, zg_ref, lb_ref, nw_ref, o_ref, s_ref, st_ref, oi_ref = refs
    c = pl.program_id(1)
    last = pl.num_programs(1) - 1
    C = zq_ref.shape[0]
    dk = st_ref.shape[2]

    @pl.when(c == 0)
    def _():
        if has_state:
            for h in range(n_heads):
                st_ref[h] = s0_ref[h].T
        else:
            st_ref[...] = jnp.zeros_like(st_ref)

    row = lax.broadcasted_iota(jnp.int32, (C, C), 0)
    col = lax.broadcasted_iota(jnp.int32, (C, C), 1)
    tril = (row >= col).astype(F32)
    for h in range(n_heads):
        sl = slice(h * dk, (h + 1) * dk)
        zq, zf, v, zg = zq_ref[:, sl], zf_ref[:, sl], zi_ref[:, sl], zg_ref[:, sl]
        lb = lb_ref[:, sl]
        q = zq * jax.nn.sigmoid(zq)
        f = lb + (1.0 - lb) * jax.nn.sigmoid(zf)
        kf = (1.0 - lb) * jax.nn.sigmoid(-zf)
        g = jnp.log(jnp.maximum(f, F_FLOOR))
        b = jnp.dot(tril, g, preferred_element_type=F32, precision=lax.Precision.HIGHEST)
        st = st_ref[h]
        o = lax.dot_general(q * jnp.exp(b), st, (((1,), (1,)), ((), ())), preferred_element_type=F32)
        for t in range(C):
            n = min(C, -(-(t + 1) // SUBLANES) * SUBLANES)
            keep = lax.broadcasted_iota(jnp.int32, (n, 1), 0) <= t
            diff = b[t:t + 1, :] - b[:n]
            e = jnp.where(keep, jnp.exp(jnp.where(keep, diff, 0.0)), 0.0)
            a_t = jnp.sum(q[t:t + 1, :] * kf[:n] * e, axis=-1, keepdims=True)
            oi_ref[pl.ds(t, 1), :] = jnp.sum(a_t * v[:n], axis=0, keepdims=True)
        o = o + oi_ref[...]
        b_last = b[C - 1:C, :]
        kdec = kf * jnp.exp(b_last - b)
        st_ref[h] = st * jnp.exp(b_last) + jnp.dot(v.T, kdec, preferred_element_type=F32,
                                                   precision=lax.Precision.HIGHEST)
        o = o * lax.rsqrt(jnp.mean(o * o, axis=-1, keepdims=True) + LN_EPS) * nw_ref[...]
        o_ref[:, sl] = (o * (zg * jax.nn.sigmoid(zg))).astype(o_ref.dtype)

    @pl.when(c == last)
    def _():
        for h in range(n_heads):
            s_ref[h] = st_ref[h].T


def hgrn_mix(z, col0, lb, nw, layer, s0, B, L, row0, n_heads, dk, dv):
    assert dk == dv and dk % LANES == 0
    width = n_heads * dk
    C = _divisor(L, HG_CHUNK)
    nc = L // C
    assert col0 % width == 0 and row0 % C == 0
    cb, rb = col0 // width, row0 // C

    def zspec(k):
        return pl.BlockSpec((C, width), lambda b, c: (rb + b * nc + c, cb + k))

    in_specs = [zspec(0), zspec(1), zspec(2), zspec(3),
                pl.BlockSpec((None, 1, width), lambda b, c: (layer, 0, 0)),
                pl.BlockSpec((None, 1, dv), lambda b, c: (layer, 0, 0))]
    args = [z, z, z, z, lb, nw]
    st_spec = pl.BlockSpec((None, n_heads, dk, dv), lambda b, c: (b, 0, 0, 0))
    if s0 is not None:
        in_specs.append(st_spec)
        args.append(s0)
    return pl.pallas_call(
        functools.partial(_hgrn_kernel, n_heads=n_heads, has_state=s0 is not None),
        grid=(B, nc),
        in_specs=in_specs,
        out_specs=[pl.BlockSpec((C, width), lambda b, c: (b * nc + c, 0)), st_spec],
        out_shape=[jax.ShapeDtypeStruct((B * L, width), F32),
                   jax.ShapeDtypeStruct((B, n_heads, dk, dv), F32)],
        scratch_shapes=[pltpu.VMEM((n_heads, dv, dk), F32), pltpu.VMEM((C, dv), F32)],
        compiler_params=_params(("arbitrary", "arbitrary")),
        name="hgrn_mix",
    )(*args)


def to_lanes(x, split):
    B, L, H, C = x.shape
    xt = jnp.transpose(x, (1, 3, 0, 2))[..., None]
    return jnp.broadcast_to(xt, (L, C, B, H, split)).reshape(L, C, B * H * split)


def vals_to_lanes(v, split):
    B, L, H, C = v.shape
    vt = jnp.transpose(v.reshape(B, L, H, split, C // split), (1, 4, 0, 2, 3))
    return vt.reshape(L, C // split, B * H * split)


def vals_from_lanes(y, B, H, split):
    L, cs, _ = y.shape
    yt = jnp.transpose(y.reshape(L, cs, B, H, split), (2, 0, 3, 4, 1))
    return yt.reshape(B, L, H * split * cs)


def _moe_kernel(be_ref, nu_ref, tok_ref, hm_ref, wp_ref, wg_ref, wl_ref, bg_ref, bl_ref, wd_ref, bd_ref,
                o_ref, xrow_ref, x_ref, sem_ref):
    i = pl.program_id(0)
    j = pl.program_id(1)
    last = pl.num_programs(1) - 1
    active = i < nu_ref[0]
    tm = x_ref.shape[0]

    def row_copies(blk, act):
        def body(r, c):
            t = tok_ref[blk * tm + r]
            act(pltpu.make_async_copy(hm_ref.at[pl.ds(t, 1)], xrow_ref.at[pl.ds(r, 1)], sem_ref.at[0]))
            return c
        lax.fori_loop(0, tm, body, 0, unroll=DMA_UNROLL)

    @pl.when(jnp.logical_and(active, j == 0))
    def _():
        @pl.when(i == 0)
        def _():
            row_copies(0, lambda cp: cp.start())

        row_copies(i, lambda cp: cp.wait())
        x_ref[...] = xrow_ref[...].astype(BF16)

        @pl.when(i + 1 < nu_ref[0])
        def _():
            row_copies(i + 1, lambda cp: cp.start())

    @pl.when(active)
    def _():
        x = x_ref[...]
        g = jnp.dot(x, wg_ref[...].astype(BF16), preferred_element_type=F32) + bg_ref[...]
        lin = jnp.dot(x, wl_ref[...].astype(BF16), preferred_element_type=F32) + bl_ref[...]
        glu = jnp.minimum(g, SWIGLU_LIMIT)
        lin = jnp.clip(lin, -SWIGLU_LIMIT, SWIGLU_LIMIT)
        act = (lin + 1.0) * glu * jax.nn.sigmoid(SWIGLU_ALPHA * glu)
        part = jnp.dot(act.astype(BF16), wd_ref[...].astype(BF16), preferred_element_type=F32)

        @pl.when(j == 0)
        def _():
            o_ref[...] = part

        @pl.when(j > 0)
        def _():
            o_ref[...] += part

        @pl.when(j == last)
        def _():
            o_ref[...] = (o_ref[...] + bd_ref[...]) * wp_ref[...]

    @pl.when(jnp.logical_and(jnp.logical_not(active), j == last))
    def _():
        o_ref[...] = jnp.zeros_like(o_ref)


def moe_experts(block_e, n_used, tok_pad, hm, w_pad, w_gu, b_gu, w_down, b_down, layer):
    rows = tok_pad.shape[0]
    D = hm.shape[1]
    d_ff = w_down.shape[2]
    tm, tf = MOE_TM, MOE_TF
    nff = d_ff // tf
    n_blocks = rows // tm

    def hold(i, j, nu):
        return jnp.where(i < nu[0], j, nff - 1)

    def row_blk(i, nu):
        return jnp.minimum(i, jnp.maximum(nu[0] - 1, 0))

    grid_spec = pltpu.PrefetchScalarGridSpec(
        num_scalar_prefetch=3,
        grid=(n_blocks, nff),
        in_specs=[
            pl.BlockSpec(memory_space=pl.ANY),
            pl.BlockSpec((tm, 1), lambda i, j, be, nu, tok: (row_blk(i, nu), 0)),
            pl.BlockSpec((None, None, D, tf), lambda i, j, be, nu, tok: (layer, be[i], 0, hold(i, j, nu))),
            pl.BlockSpec((None, None, D, tf), lambda i, j, be, nu, tok: (layer, be[i], 0, hold(i, j, nu) + nff)),
            pl.BlockSpec((None, None, 1, tf), lambda i, j, be, nu, tok: (layer, be[i], 0, hold(i, j, nu))),
            pl.BlockSpec((None, None, 1, tf), lambda i, j, be, nu, tok: (layer, be[i], 0, hold(i, j, nu) + nff)),
            pl.BlockSpec((None, None, tf, D), lambda i, j, be, nu, tok: (layer, be[i], hold(i, j, nu), 0)),
            pl.BlockSpec((None, None, 1, D), lambda i, j, be, nu, tok: (layer, be[i], 0, 0)),
        ],
        out_specs=pl.BlockSpec((tm, D), lambda i, j, be, nu, tok: (i, 0)),
        scratch_shapes=[pltpu.VMEM((tm, D), F32), pltpu.VMEM((tm, D), BF16), pltpu.SemaphoreType.DMA((1,))],
    )
    return pl.pallas_call(
        _moe_kernel,
        grid_spec=grid_spec,
        out_shape=jax.ShapeDtypeStruct((rows, D), F32),
        compiler_params=_params(("arbitrary", "arbitrary")),
        name="moe_experts",
    )(block_e, n_used, tok_pad, hm, w_pad, w_gu, w_gu, b_gu, b_gu, w_down, b_down)


def moe(hm, logits, w_gu, b_gu, w_down, b_down, layer):
    T, D = hm.shape
    E = logits.shape[-1]
    blk = MOE_TM
    top_val, top_idx = lax.top_k(logits, TOP_K)
    gates = jax.nn.softmax(top_val, axis=-1)
    TK = T * TOP_K
    flat_e = top_idx.reshape(TK).astype(jnp.int32)
    order = jnp.argsort(flat_e)
    e_sorted = flat_e[order]
    tok_sorted = (order // TOP_K).astype(jnp.int32)
    w_sorted = gates.reshape(TK)[order]
    counts = jnp.bincount(flat_e, length=E).astype(jnp.int32)
    padded = (counts + blk - 1) // blk * blk
    pad_end = jnp.cumsum(padded)
    dest = (pad_end - padded)[e_sorted] + jnp.arange(TK, dtype=jnp.int32) - (jnp.cumsum(counts) - counts)[e_sorted]
    n_blocks = -(-(TK + E * (blk - 1)) // blk)
    rows = n_blocks * blk
    tok_pad = jnp.zeros((rows,), jnp.int32).at[dest].set(tok_sorted)
    w_pad = jnp.zeros((rows,), F32).at[dest].set(w_sorted)
    block_e = jnp.minimum(jnp.searchsorted(pad_end, jnp.arange(n_blocks, dtype=jnp.int32) * blk, side='right'),
                          E - 1).astype(jnp.int32)
    n_used = (pad_end[-1] // blk).astype(jnp.int32).reshape(1)
    yb = moe_experts(block_e, n_used, tok_pad, hm, w_pad[:, None], w_gu, b_gu, w_down, b_down, layer)
    pos = jnp.zeros((TK,), jnp.int32).at[order].set(dest)
    return yb, pos


def kernel(x_prompt, x_sample, c_prompt, c_sample, state_rwkv, state_hgrn, state_shift, w_ada, b_ada, w_in, w_out, mu_rkv, mu_in, w0, w1, w2, a0, a1, a2, v0, v1, v2, mu_vg, g1, g2, k_k, k_a, r_k, lnx_g, lnx_b, hg_lower, hg_norm_w, ln1_g, ln1_b, ln2_g, ln2_b, w_router, b_router, w_gu, b_gu, w_down, b_down):
    Bp, Lp, D = x_prompt.shape
    Bs, Ls, _ = x_sample.shape
    depth = w_in.shape[0]
    d_a = w0.shape[1]
    rw_heads, rw_head = r_k.shape[1], r_k.shape[2]
    hg_dv = hg_norm_w.shape[1]
    nk = hg_lower.shape[1]
    d_b = D - d_a
    hg_heads = d_b // hg_dv
    hg_dk = nk // hg_heads
    nv = hg_heads * hg_dv
    n_exp = w_router.shape[-1]
    d_ff = w_down.shape[2]
    assert Lp % GROUP == 0 and Ls % GROUP == 0
    Tp, Ts = Bp * Lp, Bs * Ls
    T = Tp + Ts
    dn_alpha = (2 * depth) ** 0.25

    def split_groups(z):
        return z[:Tp].reshape(Bp, Lp, -1), z[Tp:].reshape(Bs, Ls, -1)

    def shift_tokens(z, prev_s):
        zp, zs = split_groups(z)
        zp = jnp.concatenate([jnp.zeros_like(zp[:, :1]), zp[:, :-1]], axis=1)
        zs = jnp.concatenate([prev_s[:, None], zs[:, :-1]], axis=1)
        return jnp.concatenate([zp.reshape(Tp, -1), zs.reshape(Ts, -1)])

    c_all = jax.nn.silu(jnp.concatenate([c_prompt, c_sample]))
    n_c = Bp + Bs
    c_pad = jnp.pad(c_all, ((0, (-n_c) % SUBLANES), (0, 0)))
    b_ada3 = b_ada[:, None, :]
    rep = jnp.concatenate([jnp.repeat(jnp.arange(Bp), Lp // GROUP), Bp + jnp.repeat(jnp.arange(Bs), Ls // GROUP)])

    p_lb = jax.nn.softmax(hg_lower.astype(F32), axis=0)
    lb_all3 = (jnp.cumsum(p_lb, axis=0) - p_lb[0])[:, None, :]
    hg_norm_w3 = hg_norm_w[:, None, :]
    assert nk == nv

    x = jnp.concatenate([x_prompt.reshape(Tp, D), x_sample.reshape(Ts, D)])
    ln1_g3, ln1_b3, ln2_g3, ln2_b3 = (t[:, None, :] for t in (ln1_g, ln1_b, ln2_g, ln2_b))
    b_router3 = b_router[:, None, :]
    b_gu4 = b_gu[:, :, None, :]
    b_down4 = b_down[:, :, None, :]

    v_first = None
    rw_p, rw_s, hg_p, hg_s, sh_p, sh_s = [], [], [], [], [], []
    for l in range(depth):
        mod_c = mm(c_pad, w_ada, l, b_ada3, name="ada")[:n_c]
        mod = mod_c[rep][:, None, :]
        SH1, SC1, GT1, SH2, SC2, GT2 = range(6)

        xm = modulate(x, mod, SC1, SH1)
        prev = state_shift[l]
        prev_bf = prev.astype(BF16)

        def last_rows(xg, m):
            sh1, sc1 = m[:, SH1 * D:(SH1 + 1) * D], m[:, SC1 * D:(SC1 + 1) * D]
            return xg[:, -1] * (1 + sc1) + sh1
        xp3, xs3 = split_groups(x)
        sh_p.append(last_rows(xp3, mod_c[:Bp]))
        sh_s.append(last_rows(xs3, mod_c[Bp:]))

        z = mm(xm, w_in, l, name="w_in")
        z_prev = mm(prev_bf, w_in, l, n_cols=3 * d_a, name="w_in_prev")

        lora_w = [w1[l], a1[l], g1[l]] + ([v1[l - 1]] if l > 0 else [])
        lora_mu = [mu_in[l, 0], mu_in[l, 1], mu_in[l, 2]] + ([mu_vg[l - 1]] if l > 0 else [])
        wl = jnp.concatenate(lora_w + [m[:, None] * w for m, w in zip(lora_mu, lora_w)], axis=1)
        nl = wl.shape[1] // 2
        zl = mm(xm, wl, name="lora_in")
        zl_prev = mm(prev_bf, wl, name="lora_in_prev")
        lora = zl[:, :nl] + shift_tokens(zl[:, nl:], zl_prev[:, nl:]) - zl[:, nl:]
        offs = [0]
        for w in lora_w:
            offs.append(offs[-1] + w.shape[1])
        lw, la, lg = (lora[:, offs[i]:offs[i + 1]] for i in range(3))

        zA = z[:, :3 * d_a]
        zA = zA + (shift_tokens(zA, z_prev) - zA) * mu_rkv[l]
        r, k, v = jnp.split(zA, 3, axis=-1)
        w_log = -jax.nn.softplus(-(w0[l] + mm(jnp.tanh(lw), w2, l, name="lora_w"))) - 0.5
        if l == 0:
            v_first = v
        else:
            lv = lora[:, offs[3]:offs[4]]
            v = v + (v_first - v) * jax.nn.sigmoid(v0[l - 1] + mm(lv, v2, l - 1, name="lora_v"))
        a = jax.nn.sigmoid(a0[l] + mm(la, a2, l, name="lora_a"))
        g = mm(jax.nn.sigmoid(lg), g2, l, name="lora_g")
        heads = lambda t: t.reshape(T, rw_heads, rw_head)
        kk = heads(k * k_k[l])
        kk = kk / jnp.maximum(jnp.sqrt(jnp.sum(kk * kk, -1, keepdims=True)), 1e-12)
        k = k * (1 + (a - 1) * k_a[l])
        decay = jnp.exp(-jnp.exp(w_log))
        a_vec = (-kk).reshape(T, d_a)
        b_vec = (kk * heads(a)).reshape(T, d_a)

        y_groups, s_groups = [], []
        for gi, (B, L, s0) in enumerate(((Bp, Lp, None), (Bs, Ls, state_rwkv[l]))):
            pick = lambda t: split_groups(t)[gi].reshape(B, L, rw_heads, rw_head)
            ins = [to_lanes(pick(t), RW_SPLIT) for t in (a_vec, decay, b_vec, k, r)]
            vl = vals_to_lanes(pick(v), RW_SPLIT)
            vr = rw_head // RW_SPLIT
            nlanes = B * rw_heads * RW_SPLIT
            if s0 is None:
                s0l = jnp.zeros((vr, rw_head, nlanes), F32)
            else:
                s0l = jnp.transpose(s0.reshape(B, rw_heads, RW_SPLIT, vr, rw_head), (3, 4, 0, 1, 2)).reshape(vr, rw_head, nlanes)
            yl, sl = rwkv_scan(*ins, vl, s0l)
            y_groups.append(vals_from_lanes(yl, B, rw_heads, RW_SPLIT).reshape(B * L, d_a))
            s_groups.append(jnp.transpose(sl.reshape(vr, rw_head, B, rw_heads, RW_SPLIT), (2, 3, 4, 0, 1)).reshape(B, rw_heads, rw_head, rw_head))
        rw_p.append(s_groups[0])
        rw_s.append(s_groups[1])
        y = heads(jnp.concatenate(y_groups))
        mu_y = jnp.mean(y, -1, keepdims=True)
        var_y = jnp.mean(jnp.square(y - mu_y), -1, keepdims=True)
        y = ((y - mu_y) * lax.rsqrt(var_y + RW_LN_EPS)).reshape(T, d_a) * lnx_g[l] + lnx_b[l]
        bonus = (jnp.sum(heads(r) * heads(k) * r_k[l], -1, keepdims=True) * heads(v)).reshape(T, d_a)
        oA = (y + bonus) * g

        oB_p, s_p = hgrn_mix(z, 3 * d_a, lb_all3, hg_norm_w3, l, None, Bp, Lp, 0, hg_heads, hg_dk, hg_dv)
        oB_s, s_s = hgrn_mix(z, 3 * d_a, lb_all3, hg_norm_w3, l, state_hgrn[l], Bs, Ls, Tp, hg_heads, hg_dk, hg_dv)
        hg_p.append(s_p)
        hg_s.append(s_s)
        oB = jnp.concatenate([oB_p, oB_s])

        mix = mm(jnp.concatenate([oA, oB], axis=-1).astype(BF16), w_out, l, name="w_out")
        h, hm, logits = ln_router(x, mix, mod, GT1, SC2, SH2, ln1_g3, ln1_b3, w_router, b_router3, l, dn_alpha)
        yb, pos = moe(hm, logits, w_gu, b_gu4, w_down, b_down4, l)
        x = combine_ln(h, yb, pos, mod, GT2, ln2_g3, ln2_b3, l, dn_alpha)

    y_prompt = x[:Tp].reshape(Bp, Lp, D)
    y_sample = x[Tp:].reshape(Bs, Ls, D)
    return (y_prompt, y_sample, jnp.stack(rw_p), jnp.stack(rw_s), jnp.stack(hg_p), jnp.stack(hg_s),
            jnp.stack(sh_p), jnp.stack(sh_s))
```

```python
import functools

import jax
import jax.numpy as jnp
from jax import lax
from jax.experimental import pallas as pl
from jax.experimental.pallas import tpu as pltpu

F32 = jnp.float32
BF16 = jnp.bfloat16

TOP_K = 4
SWIGLU_LIMIT = 7.0
SWIGLU_ALPHA = 1.702
LN_EPS = 1e-5
RW_LN_EPS = 64e-5
F_FLOOR = 1e-30

LANES = 128
SUBLANES = 8
GROUP = SUBLANES
VMEM_LIMIT = 56 * 1024 * 1024

MM_TM = 1024
MM_TN = 1024
EW_GROUPS = 64
MOE_TM = 256
MOE_TF = 512
MOE_RUN = 6
MOE_VMEM_LIMIT = 60 * 1024 * 1024
COMBINE_TT = 64
DMA_UNROLL = 8
HG_CHUNK = 32
SEQ_TB = 32


def _divisor(n, pref):
    d = min(n, pref)
    while n % d:
        d -= 1
    return d


def _params(sem):
    return pltpu.CompilerParams(dimension_semantics=sem, vmem_limit_bytes=VMEM_LIMIT)


def _mm_kernel(x_ref, w_ref, o_ref, wbf_ref):
    @pl.when(pl.program_id(1) == 0)
    def _():
        wbf_ref[...] = w_ref[...].astype(BF16)

    o_ref[...] = jnp.dot(x_ref[...].astype(BF16), wbf_ref[...], preferred_element_type=F32)


def _mm_bias_kernel(x_ref, w_ref, b_ref, o_ref, wbf_ref):
    @pl.when(pl.program_id(1) == 0)
    def _():
        wbf_ref[...] = w_ref[...].astype(BF16)

    o_ref[...] = jnp.dot(x_ref[...].astype(BF16), wbf_ref[...], preferred_element_type=F32) + b_ref[...]


def mm(x, w, layer=None, bias=None, *, n_cols=None, name="mm"):
    M, K = x.shape
    N = w.shape[-1] if n_cols is None else n_cols
    tm = MM_TM if M % MM_TM == 0 else M
    tn = MM_TN if N % MM_TN == 0 else N
    grid = (N // tn, M // tm)
    if layer is None:
        w_spec = pl.BlockSpec((K, tn), lambda j, i: (0, j))
    else:
        w_spec = pl.BlockSpec((None, K, tn), lambda j, i: (layer, 0, j))
    in_specs = [pl.BlockSpec((tm, K), lambda j, i: (i, 0)), w_spec]
    args = [x, w]
    kern = _mm_kernel
    if bias is not None:
        if layer is None:
            in_specs.append(pl.BlockSpec((1, tn), lambda j, i: (0, j)))
        else:
            in_specs.append(pl.BlockSpec((None, 1, tn), lambda j, i: (layer, 0, j)))
        args.append(bias)
        kern = _mm_bias_kernel
    return pl.pallas_call(
        kern,
        grid=grid,
        in_specs=in_specs,
        out_specs=pl.BlockSpec((tm, tn), lambda j, i: (i, j)),
        out_shape=jax.ShapeDtypeStruct((M, N), F32),
        scratch_shapes=[pltpu.VMEM((K, tn), BF16)],
        compiler_params=_params(("arbitrary", "arbitrary")),
        name=name,
    )(*args)


def _modulate_kernel(x_ref, sc_ref, sh_ref, o_ref):
    gb = sc_ref.shape[0]
    d = x_ref.shape[-1]
    x = x_ref[...].reshape(gb, GROUP, d)
    xm = x * (1.0 + sc_ref[...]) + sh_ref[...]
    o_ref[...] = xm.reshape(gb * GROUP, d).astype(o_ref.dtype)


def modulate(x, mod, sc_idx, sh_idx):
    T, D = x.shape
    G = T // GROUP
    gb = _divisor(G, EW_GROUPS)
    return pl.pallas_call(
        _modulate_kernel,
        grid=(G // gb,),
        in_specs=[
            pl.BlockSpec((gb * GROUP, D), lambda i: (i, 0)),
            pl.BlockSpec((gb, 1, D), lambda i: (i, 0, sc_idx)),
            pl.BlockSpec((gb, 1, D), lambda i: (i, 0, sh_idx)),
        ],
        out_specs=pl.BlockSpec((gb * GROUP, D), lambda i: (i, 0)),
        out_shape=jax.ShapeDtypeStruct((T, D), BF16),
        compiler_params=_params(("arbitrary",)),
        name="modulate",
    )(x, mod, mod)


def _deepnorm(x_ref, y_ref, gt_ref, g_ref, b_ref, alpha):
    gb = gt_ref.shape[0]
    d = x_ref.shape[-1]
    x = x_ref[...].reshape(gb, GROUP, d)
    y = y_ref[...].reshape(gb, GROUP, d)
    u = alpha * x + (1.0 + gt_ref[...]) * y
    mu = jnp.mean(u, axis=-1, keepdims=True)
    uc = u - mu
    var = jnp.mean(uc * uc, axis=-1, keepdims=True)
    return uc * lax.rsqrt(var + LN_EPS) * g_ref[...] + b_ref[...]


def _ln_router_kernel(x_ref, y_ref, gt_ref, g_ref, b_ref, sc_ref, sh_ref, wr_ref, br_ref,
                      h_ref, hm_ref, lg_ref, *, alpha):
    gb = gt_ref.shape[0]
    d = x_ref.shape[-1]
    h = _deepnorm(x_ref, y_ref, gt_ref, g_ref, b_ref, alpha)
    hm = (h * (1.0 + sc_ref[...]) + sh_ref[...]).reshape(gb * GROUP, d)
    h_ref[...] = h.reshape(gb * GROUP, d)
    hm_ref[...] = hm
    lg_ref[...] = jnp.dot(hm, wr_ref[...], preferred_element_type=F32,
                          precision=lax.Precision.HIGHEST) + br_ref[...]


def ln_router(x, y, mod, gt_idx, sc_idx, sh_idx, g, b, w_router, b_router, layer, alpha):
    T, D = x.shape
    G = T // GROUP
    gb = _divisor(G, EW_GROUPS)
    E = w_router.shape[-1]
    rows = pl.BlockSpec((gb * GROUP, D), lambda i: (i, 0))
    vec = pl.BlockSpec((None, 1, D), lambda i: (layer, 0, 0))
    return pl.pallas_call(
        functools.partial(_ln_router_kernel, alpha=alpha),
        grid=(G // gb,),
        in_specs=[
            rows, rows,
            pl.BlockSpec((gb, 1, D), lambda i: (i, 0, gt_idx)),
            vec, vec,
            pl.BlockSpec((gb, 1, D), lambda i: (i, 0, sc_idx)),
            pl.BlockSpec((gb, 1, D), lambda i: (i, 0, sh_idx)),
            pl.BlockSpec((None, D, E), lambda i: (layer, 0, 0)),
            pl.BlockSpec((None, 1, E), lambda i: (layer, 0, 0)),
        ],
        out_specs=[rows, rows, pl.BlockSpec((gb * GROUP, E), lambda i: (i, 0))],
        out_shape=[
            jax.ShapeDtypeStruct((T, D), F32),
            jax.ShapeDtypeStruct((T, D), F32),
            jax.ShapeDtypeStruct((T, E), F32),
        ],
        compiler_params=_params(("arbitrary",)),
        name="ln_router",
    )(x, y, mod, g, b, mod, mod, w_router, b_router)


def _combine_ln_kernel(pos_ref, h_ref, yb_ref, gates_ref, gt_ref, g_ref, b_ref, o_ref, buf_ref, sem_ref, *, alpha):
    i = pl.program_id(0)
    n = pl.num_programs(0)
    tt = h_ref.shape[0]

    def row_copies(tile, slot, act):
        for j in range(TOP_K):
            def body(tok, c):
                p = pos_ref[(tile * tt + tok) * TOP_K + j]
                act(pltpu.make_async_copy(yb_ref.at[pl.ds(p, 1)],
                                          buf_ref.at[slot, pl.ds(j * tt + tok, 1)],
                                          sem_ref.at[slot]))
                return c
            lax.fori_loop(0, tt, body, 0, unroll=DMA_UNROLL)

    @pl.when(i == 0)
    def _():
        row_copies(0, 0, lambda cp: cp.start())

    @pl.when(i + 1 < n)
    def _():
        row_copies(i + 1, (i + 1) % 2, lambda cp: cp.start())

    slot = i % 2
    row_copies(i, slot, lambda cp: cp.wait())
    gates = gates_ref[...]
    y = buf_ref[slot, pl.ds(0, tt), :] * gates[:, 0:1]
    for j in range(1, TOP_K):
        y = y + buf_ref[slot, pl.ds(j * tt, tt), :] * gates[:, j:j + 1]
    gb = gt_ref.shape[0]
    d = h_ref.shape[-1]
    u = alpha * h_ref[...].reshape(gb, GROUP, d) + (1.0 + gt_ref[...]) * y.reshape(gb, GROUP, d)
    mu = jnp.mean(u, axis=-1, keepdims=True)
    uc = u - mu
    var = jnp.mean(uc * uc, axis=-1, keepdims=True)
    o_ref[...] = (uc * lax.rsqrt(var + LN_EPS) * g_ref[...] + b_ref[...]).reshape(gb * GROUP, d)


def combine_ln(h, yb, pos, gates, mod, gt_idx, g, b, layer, alpha):
    T, D = h.shape
    tt = COMBINE_TT
    gb = tt // GROUP
    assert T % tt == 0
    grid_spec = pltpu.PrefetchScalarGridSpec(
        num_scalar_prefetch=1,
        grid=(T // tt,),
        in_specs=[
            pl.BlockSpec((tt, D), lambda i, pos: (i, 0)),
            pl.BlockSpec(memory_space=pl.ANY),
            pl.BlockSpec((tt, TOP_K), lambda i, pos: (i, 0)),
            pl.BlockSpec((gb, 1, D), lambda i, pos: (i, 0, gt_idx)),
            pl.BlockSpec((None, 1, D), lambda i, pos: (layer, 0, 0)),
            pl.BlockSpec((None, 1, D), lambda i, pos: (layer, 0, 0)),
        ],
        out_specs=pl.BlockSpec((tt, D), lambda i, pos: (i, 0)),
        scratch_shapes=[pltpu.VMEM((2, TOP_K * tt, D), F32), pltpu.SemaphoreType.DMA((2,))],
    )
    return pl.pallas_call(
        functools.partial(_combine_ln_kernel, alpha=alpha),
        grid_spec=grid_spec,
        out_shape=jax.ShapeDtypeStruct((T, D), F32),
        compiler_params=_params(("arbitrary",)),
        name="combine_ln",
    )(pos, h, yb, gates, mod, g, b)


def _rwkv_kernel(r_ref, k_ref, a_ref, w_ref, v_ref, kk_ref, ka_ref, s0_ref, y_ref, s_ref,
                 sa_ref, yp_ref, pre_ref):
    @pl.when(pl.program_id(1) == 0)
    def _():
        s_ref[...] = s0_ref[...]

    n_steps = r_ref.shape[0]
    n_rows = v_ref.shape[1]

    def both_halves(x):
        return x + pltpu.roll(x, LANES // 2, axis=1)

    def prepare(t, carry):
        kr = k_ref[t]
        gate = jax.nn.sigmoid(a_ref[t])
        wp = w_ref[t]
        w_log = -(jnp.maximum(-wp, 0.0) + jnp.log(1.0 + jnp.exp(-jnp.abs(wp)))) - 0.5
        kk = kr * kk_ref[...]
        norm = jnp.sqrt(both_halves(jnp.sum(kk * kk, axis=0, keepdims=True)))
        kk = kk / jnp.maximum(norm, 1e-12)
        pre_ref[t, 0] = jnp.exp(-jnp.exp(w_log))
        pre_ref[t, 1] = -kk
        pre_ref[t, 2] = kk * gate
        pre_ref[t, 3] = kr * (1.0 + (gate - 1.0) * ka_ref[...])
        return carry

    lax.fori_loop(0, n_steps, prepare, 0, unroll=2)

    def step(t, carry):
        r = r_ref[t]
        d = pre_ref[t, 0]
        av = pre_ref[t, 1]
        bv = pre_ref[t, 2]
        km = pre_ref[t, 3]
        for v in range(n_rows):
            sa_ref[pl.ds(v, 1), :] = jnp.sum(s_ref[v] * av, axis=0, keepdims=True)
        sa_ref[...] = both_halves(sa_ref[...])
        for v in range(n_rows):
            s = s_ref[v] * d + sa_ref[pl.ds(v, 1), :] * bv + v_ref[t, pl.ds(v, 1), :] * km
            s_ref[v] = s
            yp_ref[pl.ds(v, 1), :] = jnp.sum(s * r, axis=0, keepdims=True)
        y_ref[t] = both_halves(yp_ref[...])
        return carry

    lax.fori_loop(0, n_steps, step, 0)


def rwkv_scan(r, k, a, w, v, kk, ka, s0):
    L, KH, NL = r.shape
    V = v.shape[1]
    tb = SEQ_TB if L % SEQ_TB == 0 else L
    vec = pl.BlockSpec((tb, KH, LANES), lambda g, t: (t, 0, g))
    val = pl.BlockSpec((tb, V, LANES), lambda g, t: (t, 0, g))
    par = pl.BlockSpec((KH, LANES), lambda g, t: (0, g))
    st = pl.BlockSpec((V, KH, LANES), lambda g, t: (0, 0, g))
    return pl.pallas_call(
        _rwkv_kernel,
        grid=(NL // LANES, L // tb),
        in_specs=[vec, vec, vec, vec, val, par, par, st],
        out_specs=[val, st],
        out_shape=[jax.ShapeDtypeStruct((L, V, NL), F32), jax.ShapeDtypeStruct((V, KH, NL), F32)],
        scratch_shapes=[pltpu.VMEM((V, LANES), F32), pltpu.VMEM((V, LANES), F32),
                        pltpu.VMEM((tb, 4, KH, LANES), F32)],
        compiler_params=_params(("arbitrary", "arbitrary")),
        name="rwkv_scan",
    )(r, k, a, w, v, kk, ka, s0)


def keys_to_lanes(x):
    B, L, H, K = x.shape
    half = LANES // 2
    xt = jnp.transpose(x.reshape(B, L, H, 2, K // 2), (1, 4, 0, 2, 3))
    xt = jnp.swapaxes(xt.reshape(L, K // 2, (B * H) // half, half, 2), 3, 4)
    return xt.reshape(L, K // 2, 2 * B * H)


def vals_to_lanes(v):
    B, L, H, V = v.shape
    half = LANES // 2
    vt = jnp.transpose(v, (1, 3, 0, 2)).reshape(L, V, (B * H) // half, 1, half)
    return jnp.broadcast_to(vt, (L, V, (B * H) // half, 2, half)).reshape(L, V, 2 * B * H)


def vals_from_lanes(y, B, H):
    L, V, NL = y.shape
    half = LANES // 2
    yt = y.reshape(L, V, NL // LANES, 2, half)[:, :, :, 0, :].reshape(L, V, B, H)
    return jnp.transpose(yt, (2, 0, 3, 1)).reshape(B, L, H * V)


def state_to_lanes(s):
    B, H, V, K = s.shape
    half = LANES // 2
    st = jnp.transpose(s.reshape(B, H, V, 2, K // 2), (2, 4, 0, 1, 3))
    st = jnp.swapaxes(st.reshape(V, K // 2, (B * H) // half, half, 2), 3, 4)
    return st.reshape(V, K // 2, 2 * B * H)


def state_from_lanes(s, B, H):
    V, KH, NL = s.shape
    half = LANES // 2
    st = jnp.swapaxes(s.reshape(V, KH, NL // LANES, 2, half), 3, 4).reshape(V, KH, B, H, 2)
    return jnp.transpose(st, (2, 3, 0, 4, 1)).reshape(B, H, V, 2 * KH)


def head_param_to_lanes(p, B):
    H, K = p.shape
    half = LANES // 2
    pt = jnp.broadcast_to(jnp.transpose(p.reshape(H, 2, K // 2), (2, 0, 1))[:, None], (K // 2, B, H, 2))
    pt = jnp.swapaxes(pt.reshape(K // 2, (B * H) // half, half, 2), 2, 3)
    return pt.reshape(K // 2, 2 * B * H)


def _hgrn_kernel(*refs, n_heads, has_state):
    if has_state:
        zq_ref, zf_ref, zi_ref, zg_ref, lb_ref, nw_ref, s0_ref, o_ref, s_ref, st_ref, oi_ref = refs
    else:
        zq_ref, zf_ref, zi_ref, zg_ref, lb_ref, nw_ref, o_ref, s_ref, st_ref, oi_ref = refs
    c = pl.program_id(1)
    last = pl.num_programs(1) - 1
    C = zq_ref.shape[0]
    dk = st_ref.shape[2]

    @pl.when(c == 0)
    def _():
        if has_state:
            for h in range(n_heads):
                st_ref[h] = s0_ref[h].T
        else:
            st_ref[...] = jnp.zeros_like(st_ref)

    row = lax.broadcasted_iota(jnp.int32, (C, C), 0)
    col = lax.broadcasted_iota(jnp.int32, (C, C), 1)
    tril = (row >= col).astype(F32)
    for h in range(n_heads):
        sl = slice(h * dk, (h + 1) * dk)
        zq, zf, v, zg = zq_ref[:, sl], zf_ref[:, sl], zi_ref[:, sl], zg_ref[:, sl]
        lb = lb_ref[:, sl]
        q = zq * jax.nn.sigmoid(zq)
        f = lb + (1.0 - lb) * jax.nn.sigmoid(zf)
        kf = (1.0 - lb) * jax.nn.sigmoid(-zf)
        g = jnp.log(jnp.maximum(f, F_FLOOR))
        b = jnp.dot(tril, g, preferred_element_type=F32, precision=lax.Precision.HIGHEST)
        st = st_ref[h]
        o = lax.dot_general(q * jnp.exp(b), st, (((1,), (1,)), ((), ())), preferred_element_type=F32)
        for t in range(C):
            n = min(C, -(-(t + 1) // SUBLANES) * SUBLANES)
            keep = lax.broadcasted_iota(jnp.int32, (n, 1), 0) <= t
            diff = b[t:t + 1, :] - b[:n]
            e = jnp.where(keep, jnp.exp(jnp.where(keep, diff, 0.0)), 0.0)
            a_t = jnp.sum(q[t:t + 1, :] * kf[:n] * e, axis=-1, keepdims=True)
            oi_ref[pl.ds(t, 1), :] = jnp.sum(a_t * v[:n], axis=0, keepdims=True)
        o = o + oi_ref[...]
        b_last = b[C - 1:C, :]
        kdec = kf * jnp.exp(b_last - b)
        st_ref[h] = st * jnp.exp(b_last) + jnp.dot(v.T, kdec, preferred_element_type=F32,
                                                   precision=lax.Precision.HIGHEST)
        o = o * lax.rsqrt(jnp.mean(o * o, axis=-1, keepdims=True) + LN_EPS) * nw_ref[...]
        o_ref[:, sl] = (o * (zg * jax.nn.sigmoid(zg))).astype(o_ref.dtype)

    @pl.when(c == last)
    def _():
        for h in range(n_heads):
            s_ref[h] = st_ref[h].T


def hgrn_mix(z, col0, lb, nw, layer, s0, B, L, row0, n_heads, dk, dv):
    assert dk == dv and dk % LANES == 0
    width = n_heads * dk
    C = _divisor(L, HG_CHUNK)
    nc = L // C
    assert col0 % width == 0 and row0 % C == 0
    cb, rb = col0 // width, row0 // C

    def zspec(k):
        return pl.BlockSpec((C, width), lambda b, c: (rb + b * nc + c, cb + k))

    in_specs = [zspec(0), zspec(1), zspec(2), zspec(3),
                pl.BlockSpec((None, 1, width), lambda b, c: (layer, 0, 0)),
                pl.BlockSpec((None, 1, dv), lambda b, c: (layer, 0, 0))]
    args = [z, z, z, z, lb, nw]
    st_spec = pl.BlockSpec((None, n_heads, dk, dv), lambda b, c: (b, 0, 0, 0))
    if s0 is not None:
        in_specs.append(st_spec)
        args.append(s0)
    return pl.pallas_call(
        functools.partial(_hgrn_kernel, n_heads=n_heads, has_state=s0 is not None),
        grid=(B, nc),
        in_specs=in_specs,
        out_specs=[pl.BlockSpec((C, width), lambda b, c: (b * nc + c, 0)), st_spec],
        out_shape=[jax.ShapeDtypeStruct((B * L, width), F32),
                   jax.ShapeDtypeStruct((B, n_heads, dk, dv), F32)],
        scratch_shapes=[pltpu.VMEM((n_heads, dv, dk), F32), pltpu.VMEM((C, dv), F32)],
        compiler_params=_params(("arbitrary", "arbitrary")),
        name="hgrn_mix",
    )(*args)


def _moe_kernel(ue_ref, ub_ref, ul_ref, nu_ref, tok_ref, hm_ref, wg_ref, wl_ref, bg_ref, bl_ref, wd_ref, bd_ref,
                yb_ref, xrow_ref, x_ref, acc_ref, gsem_ref, osem_ref):
    u = pl.program_id(0)
    j = pl.program_id(1)
    last = pl.num_programs(1) - 1
    tm = xrow_ref.shape[0]
    n_units, n_used = nu_ref[0], nu_ref[1]
    blk0, nblk = ub_ref[u], ul_ref[u]

    def row_copies(blk, act):
        def body(r, c):
            t = tok_ref[blk * tm + r]
            act(pltpu.make_async_copy(hm_ref.at[pl.ds(t, 1)], xrow_ref.at[pl.ds(r, 1)], gsem_ref.at[0]))
            return c
        lax.fori_loop(0, tm, body, 0, unroll=DMA_UNROLL)

    def out_copy(r):
        return pltpu.make_async_copy(acc_ref.at[r], yb_ref.at[pl.ds((blk0 + r) * tm, tm)], osem_ref.at[0])

    @pl.when(u < n_units)
    def _():
        wg = wg_ref[...].astype(BF16)
        wl = wl_ref[...].astype(BF16)
        wd = wd_ref[...].astype(BF16)

        def block(r, c):
            @pl.when(j == 0)
            def _():
                b = blk0 + r

                @pl.when(b == 0)
                def _():
                    row_copies(0, lambda cp: cp.start())

                row_copies(b, lambda cp: cp.wait())
                x_ref[r] = xrow_ref[...].astype(BF16)

                @pl.when(b + 1 < n_used)
                def _():
                    row_copies(b + 1, lambda cp: cp.start())

            x = x_ref[r]
            g = jnp.dot(x, wg, preferred_element_type=F32) + bg_ref[...]
            lin = jnp.dot(x, wl, preferred_element_type=F32) + bl_ref[...]
            glu = jnp.minimum(g, SWIGLU_LIMIT)
            lin = jnp.clip(lin, -SWIGLU_LIMIT, SWIGLU_LIMIT)
            act = (lin + 1.0) * glu * jax.nn.sigmoid(SWIGLU_ALPHA * glu)
            part = jnp.dot(act.astype(BF16), wd, preferred_element_type=F32)

            @pl.when(j == 0)
            def _():
                acc_ref[r] = part

            @pl.when(j > 0)
            def _():
                acc_ref[r] += part

            @pl.when(j == last)
            def _():
                acc_ref[r] += bd_ref[...]
                out_copy(r).start()

            return c

        lax.fori_loop(0, nblk, block, 0)

        @pl.when(j == last)
        def _():
            def drain(r, c):
                out_copy(r).wait()
                return c
            lax.fori_loop(0, nblk, drain, 0)

    @pl.when(jnp.logical_and(u == pl.num_programs(0) - 1, j == last))
    def _():
        n_blocks = yb_ref.shape[0] // tm
        xrow_ref[...] = jnp.zeros_like(xrow_ref)

        def fill(b):
            return pltpu.make_async_copy(xrow_ref, yb_ref.at[pl.ds(b * tm, tm)], osem_ref.at[0])

        def start(b, c):
            fill(b).start()
            return c

        def wait(b, c):
            fill(b).wait()
            return c

        lax.fori_loop(n_used, n_blocks, start, 0)
        lax.fori_loop(n_used, n_blocks, wait, 0)


def moe_experts(unit_e, unit_blk, unit_len, n_units_used, tok_pad, hm, w_gu, b_gu, w_down, b_down, layer):
    rows = tok_pad.shape[0]
    D = hm.shape[1]
    d_ff = w_down.shape[2]
    tm, tf = MOE_TM, MOE_TF
    nff = d_ff // tf

    def hold(u, j, nu):
        return jnp.where(u < nu[0], j, nff - 1)

    grid_spec = pltpu.PrefetchScalarGridSpec(
        num_scalar_prefetch=5,
        grid=(unit_e.shape[0], nff),
        in_specs=[
            pl.BlockSpec(memory_space=pl.ANY),
            pl.BlockSpec((None, None, D, tf), lambda u, j, ue, ub, ul, nu, tok: (layer, ue[u], 0, hold(u, j, nu))),
            pl.BlockSpec((None, None, D, tf), lambda u, j, ue, ub, ul, nu, tok: (layer, ue[u], 0, hold(u, j, nu) + nff)),
            pl.BlockSpec((None, None, 1, tf), lambda u, j, ue, ub, ul, nu, tok: (layer, ue[u], 0, hold(u, j, nu))),
            pl.BlockSpec((None, None, 1, tf), lambda u, j, ue, ub, ul, nu, tok: (layer, ue[u], 0, hold(u, j, nu) + nff)),
            pl.BlockSpec((None, None, tf, D), lambda u, j, ue, ub, ul, nu, tok: (layer, ue[u], hold(u, j, nu), 0)),
            pl.BlockSpec((None, None, 1, D), lambda u, j, ue, ub, ul, nu, tok: (layer, ue[u], 0, 0)),
        ],
        out_specs=pl.BlockSpec(memory_space=pl.ANY),
        scratch_shapes=[pltpu.VMEM((tm, D), F32), pltpu.VMEM((MOE_RUN, tm, D), BF16),
                        pltpu.VMEM((MOE_RUN, tm, D), F32),
                        pltpu.SemaphoreType.DMA((1,)), pltpu.SemaphoreType.DMA((1,))],
    )
    return pl.pallas_call(
        _moe_kernel,
        grid_spec=grid_spec,
        out_shape=jax.ShapeDtypeStruct((rows, D), F32),
        compiler_params=pltpu.CompilerParams(dimension_semantics=("arbitrary", "arbitrary"),
                                             vmem_limit_bytes=MOE_VMEM_LIMIT),
        name="moe_experts",
    )(unit_e, unit_blk, unit_len, n_units_used, tok_pad, hm, w_gu, w_gu, b_gu, b_gu, w_down, b_down)


def moe(hm, logits, w_gu, b_gu, w_down, b_down, layer):
    T, D = hm.shape
    E = logits.shape[-1]
    blk = MOE_TM
    top_val, top_idx = lax.top_k(logits, TOP_K)
    gates = jax.nn.softmax(top_val, axis=-1)
    TK = T * TOP_K
    flat_e = top_idx.reshape(TK).astype(jnp.int32)
    order = jnp.argsort(flat_e)
    e_sorted = flat_e[order]
    tok_sorted = (order // TOP_K).astype(jnp.int32)
    counts = jnp.bincount(flat_e, length=E).astype(jnp.int32)
    nblk_e = (counts + blk - 1) // blk
    blk_end = jnp.cumsum(nblk_e)
    blk_start = blk_end - nblk_e
    dest = blk_start[e_sorted] * blk + jnp.arange(TK, dtype=jnp.int32) - (jnp.cumsum(counts) - counts)[e_sorted]
    n_blocks = -(-(TK + E * (blk - 1)) // blk)
    rows = n_blocks * blk
    tok_pad = jnp.zeros((rows,), jnp.int32).at[dest].set(tok_sorted)
    nun_e = (nblk_e + MOE_RUN - 1) // MOE_RUN
    un_end = jnp.cumsum(nun_e)
    n_units = un_end[-1]
    max_units = -(-n_blocks // MOE_RUN) + E
    uid = jnp.minimum(jnp.arange(max_units, dtype=jnp.int32), n_units - 1)
    unit_e = jnp.minimum(jnp.searchsorted(un_end, uid, side='right'), E - 1).astype(jnp.int32)
    k_in_e = uid - (un_end - nun_e)[unit_e]
    unit_blk = (blk_start[unit_e] + k_in_e * MOE_RUN).astype(jnp.int32)
    unit_len = jnp.clip(nblk_e[unit_e] - k_in_e * MOE_RUN, 0, MOE_RUN)
    unit_len = jnp.where(jnp.arange(max_units) < n_units, unit_len, 0).astype(jnp.int32)
    n_used = jnp.stack([n_units, blk_end[-1]]).astype(jnp.int32)
    yb = moe_experts(unit_e, unit_blk, unit_len, n_used, tok_pad, hm, w_gu, b_gu, w_down, b_down, layer)
    pos = jnp.zeros((TK,), jnp.int32).at[order].set(dest)
    return yb, pos, gates


def kernel(x_prompt, x_sample, c_prompt, c_sample, state_rwkv, state_hgrn, state_shift, w_ada, b_ada, w_in, w_out, mu_rkv, mu_in, w0, w1, w2, a0, a1, a2, v0, v1, v2, mu_vg, g1, g2, k_k, k_a, r_k, lnx_g, lnx_b, hg_lower, hg_norm_w, ln1_g, ln1_b, ln2_g, ln2_b, w_router, b_router, w_gu, b_gu, w_down, b_down):
    Bp, Lp, D = x_prompt.shape
    Bs, Ls, _ = x_sample.shape
    depth = w_in.shape[0]
    d_a = w0.shape[1]
    rw_heads, rw_head = r_k.shape[1], r_k.shape[2]
    hg_dv = hg_norm_w.shape[1]
    nk = hg_lower.shape[1]
    d_b = D - d_a
    hg_heads = d_b // hg_dv
    hg_dk = nk // hg_heads
    nv = hg_heads * hg_dv
    n_exp = w_router.shape[-1]
    d_ff = w_down.shape[2]
    assert Lp % GROUP == 0 and Ls % GROUP == 0
    Tp, Ts = Bp * Lp, Bs * Ls
    T = Tp + Ts
    dn_alpha = (2 * depth) ** 0.25

    def split_groups(z):
        return z[:Tp].reshape(Bp, Lp, -1), z[Tp:].reshape(Bs, Ls, -1)

    def shift_tokens(z, prev_s):
        zp, zs = split_groups(z)
        zp = jnp.concatenate([jnp.zeros_like(zp[:, :1]), zp[:, :-1]], axis=1)
        zs = jnp.concatenate([prev_s[:, None], zs[:, :-1]], axis=1)
        return jnp.concatenate([zp.reshape(Tp, -1), zs.reshape(Ts, -1)])

    c_all = jax.nn.silu(jnp.concatenate([c_prompt, c_sample]))
    n_c = Bp + Bs
    c_pad = jnp.pad(c_all, ((0, (-n_c) % SUBLANES), (0, 0)))
    b_ada3 = b_ada[:, None, :]
    rep = jnp.concatenate([jnp.repeat(jnp.arange(Bp), Lp // GROUP), Bp + jnp.repeat(jnp.arange(Bs), Ls // GROUP)])

    p_lb = jax.nn.softmax(hg_lower.astype(F32), axis=0)
    lb_all3 = (jnp.cumsum(p_lb, axis=0) - p_lb[0])[:, None, :]
    hg_norm_w3 = hg_norm_w[:, None, :]
    assert nk == nv

    x = jnp.concatenate([x_prompt.reshape(Tp, D), x_sample.reshape(Ts, D)])
    ln1_g3, ln1_b3, ln2_g3, ln2_b3 = (t[:, None, :] for t in (ln1_g, ln1_b, ln2_g, ln2_b))
    b_router3 = b_router[:, None, :]
    b_gu4 = b_gu[:, :, None, :]
    b_down4 = b_down[:, :, None, :]

    v_first = None
    rw_p, rw_s, hg_p, hg_s, sh_p, sh_s = [], [], [], [], [], []
    for l in range(depth):
        mod_c = mm(c_pad, w_ada, l, b_ada3, name="ada")[:n_c]
        mod = mod_c[rep][:, None, :]
        SH1, SC1, GT1, SH2, SC2, GT2 = range(6)

        xm = modulate(x, mod, SC1, SH1)
        prev = state_shift[l]
        prev_bf = prev.astype(BF16)

        def last_rows(xg, m):
            sh1, sc1 = m[:, SH1 * D:(SH1 + 1) * D], m[:, SC1 * D:(SC1 + 1) * D]
            return xg[:, -1] * (1 + sc1) + sh1
        xp3, xs3 = split_groups(x)
        sh_p.append(last_rows(xp3, mod_c[:Bp]))
        sh_s.append(last_rows(xs3, mod_c[Bp:]))

        z = mm(xm, w_in, l, name="w_in")
        z_prev = mm(prev_bf, w_in, l, n_cols=3 * d_a, name="w_in_prev")

        lora_w = [w1[l], a1[l], g1[l]] + ([v1[l - 1]] if l > 0 else [])
        lora_mu = [mu_in[l, 0], mu_in[l, 1], mu_in[l, 2]] + ([mu_vg[l - 1]] if l > 0 else [])
        wl = jnp.concatenate(lora_w + [m[:, None] * w for m, w in zip(lora_mu, lora_w)], axis=1)
        nl = wl.shape[1] // 2
        zl = mm(xm, wl, name="lora_in")
        zl_prev = mm(prev_bf, wl, name="lora_in_prev")
        lora = zl[:, :nl] + shift_tokens(zl[:, nl:], zl_prev[:, nl:]) - zl[:, nl:]
        offs = [0]
        for w in lora_w:
            offs.append(offs[-1] + w.shape[1])
        lw, la, lg = (lora[:, offs[i]:offs[i + 1]] for i in range(3))

        zA = z[:, :3 * d_a]
        zA = zA + (shift_tokens(zA, z_prev) - zA) * mu_rkv[l]
        r, k, v = jnp.split(zA, 3, axis=-1)
        w_pre = w0[l] + mm(jnp.tanh(lw), w2, l, name="lora_w")
        if l == 0:
            v_first = v
        else:
            lv = lora[:, offs[3]:offs[4]]
            v = v + (v_first - v) * jax.nn.sigmoid(v0[l - 1] + mm(lv, v2, l - 1, name="lora_v"))
        a_pre = a0[l] + mm(la, a2, l, name="lora_a")
        g = mm(jax.nn.sigmoid(lg), g2, l, name="lora_g")
        heads = lambda t: t.reshape(T, rw_heads, rw_head)

        y_groups, s_groups = [], []
        for gi, (B, L, s0) in enumerate(((Bp, Lp, None), (Bs, Ls, state_rwkv[l]))):
            pick = lambda t: split_groups(t)[gi].reshape(B, L, rw_heads, rw_head)
            ins = [keys_to_lanes(pick(t)) for t in (r, k, a_pre, w_pre)]
            vl = vals_to_lanes(pick(v))
            nlanes = 2 * B * rw_heads
            if s0 is None:
                s0l = jnp.zeros((rw_head, rw_head // 2, nlanes), F32)
            else:
                s0l = state_to_lanes(s0)
            pk = head_param_to_lanes(k_k[l].reshape(rw_heads, rw_head), B)
            pa = head_param_to_lanes(k_a[l].reshape(rw_heads, rw_head), B)
            yl, sl = rwkv_scan(*ins, vl, pk, pa, s0l)
            y_groups.append(vals_from_lanes(yl, B, rw_heads).reshape(B * L, d_a))
            s_groups.append(state_from_lanes(sl, B, rw_heads))
        rw_p.append(s_groups[0])
        rw_s.append(s_groups[1])
        k = k * (1 + (jax.nn.sigmoid(a_pre) - 1) * k_a[l])
        y = heads(jnp.concatenate(y_groups))
        mu_y = jnp.mean(y, -1, keepdims=True)
        var_y = jnp.mean(jnp.square(y - mu_y), -1, keepdims=True)
        y = ((y - mu_y) * lax.rsqrt(var_y + RW_LN_EPS)).reshape(T, d_a) * lnx_g[l] + lnx_b[l]
        bonus = (jnp.sum(heads(r) * heads(k) * r_k[l], -1, keepdims=True) * heads(v)).reshape(T, d_a)
        oA = (y + bonus) * g

        oB_p, s_p = hgrn_mix(z, 3 * d_a, lb_all3, hg_norm_w3, l, None, Bp, Lp, 0, hg_heads, hg_dk, hg_dv)
        oB_s, s_s = hgrn_mix(z, 3 * d_a, lb_all3, hg_norm_w3, l, state_hgrn[l], Bs, Ls, Tp, hg_heads, hg_dk, hg_dv)
        hg_p.append(s_p)
        hg_s.append(s_s)
        oB = jnp.concatenate([oB_p, oB_s])

        mix = mm(jnp.concatenate([oA, oB], axis=-1).astype(BF16), w_out, l, name="w_out")
        h, hm, logits = ln_router(x, mix, mod, GT1, SC2, SH2, ln1_g3, ln1_b3, w_router, b_router3, l, dn_alpha)
        yb, pos, gates = moe(hm, logits, w_gu, b_gu4, w_down, b_down4, l)
        x = combine_ln(h, yb, pos, gates, mod, GT2, ln2_g3, ln2_b3, l, dn_alpha)

    y_prompt = x[:Tp].reshape(Bp, Lp, D)
    y_sample = x[Tp:].reshape(Bs, Ls, D)
    return (y_prompt, y_sample, jnp.stack(rw_p), jnp.stack(rw_s), jnp.stack(hg_p), jnp.stack(hg_s),
            jnp.stack(sh_p), jnp.stack(sh_s))
```

```python
import functools

import jax
import jax.numpy as jnp
from jax import lax
from jax.experimental import pallas as pl
from jax.experimental.pallas import tpu as pltpu

F32 = jnp.float32
BF16 = jnp.bfloat16

TOP_K = 4
SWIGLU_LIMIT = 7.0
SWIGLU_ALPHA = 1.702
LN_EPS = 1e-5
RW_LN_EPS = 64e-5
F_FLOOR = 1e-30

LANES = 128
SUBLANES = 8
GROUP = SUBLANES
VMEM_LIMIT = 56 * 1024 * 1024

MM_TM = 1024
MM_TN = 1024
EW_GROUPS = 64
MOE_TM = 256
MOE_TF = 256
MOE_RUN = 8
MOE_VMEM_LIMIT = 60 * 1024 * 1024
COMBINE_TT = 64
DMA_UNROLL = 8
RW_SPLIT = 2
HG_CHUNK = 32
SEQ_TB = 32


def _divisor(n, pref):
    d = min(n, pref)
    while n % d:
        d -= 1
    return d


def _params(sem):
    return pltpu.CompilerParams(dimension_semantics=sem, vmem_limit_bytes=VMEM_LIMIT)


def _mm_kernel(x_ref, w_ref, o_ref, wbf_ref):
    @pl.when(pl.program_id(1) == 0)
    def _():
        wbf_ref[...] = w_ref[...].astype(BF16)

    o_ref[...] = jnp.dot(x_ref[...].astype(BF16), wbf_ref[...], preferred_element_type=F32)


def _mm_bias_kernel(x_ref, w_ref, b_ref, o_ref, wbf_ref):
    @pl.when(pl.program_id(1) == 0)
    def _():
        wbf_ref[...] = w_ref[...].astype(BF16)

    o_ref[...] = jnp.dot(x_ref[...].astype(BF16), wbf_ref[...], preferred_element_type=F32) + b_ref[...]


def mm(x, w, layer=None, bias=None, *, n_cols=None, name="mm"):
    M, K = x.shape
    N = w.shape[-1] if n_cols is None else n_cols
    tm = MM_TM if M % MM_TM == 0 else M
    tn = MM_TN if N % MM_TN == 0 else N
    grid = (N // tn, M // tm)
    if layer is None:
        w_spec = pl.BlockSpec((K, tn), lambda j, i: (0, j))
    else:
        w_spec = pl.BlockSpec((None, K, tn), lambda j, i: (layer, 0, j))
    in_specs = [pl.BlockSpec((tm, K), lambda j, i: (i, 0)), w_spec]
    args = [x, w]
    kern = _mm_kernel
    if bias is not None:
        if layer is None:
            in_specs.append(pl.BlockSpec((1, tn), lambda j, i: (0, j)))
        else:
            in_specs.append(pl.BlockSpec((None, 1, tn), lambda j, i: (layer, 0, j)))
        args.append(bias)
        kern = _mm_bias_kernel
    return pl.pallas_call(
        kern,
        grid=grid,
        in_specs=in_specs,
        out_specs=pl.BlockSpec((tm, tn), lambda j, i: (i, j)),
        out_shape=jax.ShapeDtypeStruct((M, N), F32),
        scratch_shapes=[pltpu.VMEM((K, tn), BF16)],
        compiler_params=_params(("arbitrary", "arbitrary")),
        name=name,
    )(*args)


def _modulate_kernel(x_ref, sc_ref, sh_ref, o_ref):
    gb = sc_ref.shape[0]
    d = x_ref.shape[-1]
    x = x_ref[...].reshape(gb, GROUP, d)
    xm = x * (1.0 + sc_ref[...]) + sh_ref[...]
    o_ref[...] = xm.reshape(gb * GROUP, d).astype(o_ref.dtype)


def modulate(x, mod, sc_idx, sh_idx):
    T, D = x.shape
    G = T // GROUP
    gb = _divisor(G, EW_GROUPS)
    return pl.pallas_call(
        _modulate_kernel,
        grid=(G // gb,),
        in_specs=[
            pl.BlockSpec((gb * GROUP, D), lambda i: (i, 0)),
            pl.BlockSpec((gb, 1, D), lambda i: (i, 0, sc_idx)),
            pl.BlockSpec((gb, 1, D), lambda i: (i, 0, sh_idx)),
        ],
        out_specs=pl.BlockSpec((gb * GROUP, D), lambda i: (i, 0)),
        out_shape=jax.ShapeDtypeStruct((T, D), BF16),
        compiler_params=_params(("arbitrary",)),
        name="modulate",
    )(x, mod, mod)


def _deepnorm(x_ref, y_ref, gt_ref, g_ref, b_ref, alpha):
    gb = gt_ref.shape[0]
    d = x_ref.shape[-1]
    x = x_ref[...].reshape(gb, GROUP, d)
    y = y_ref[...].reshape(gb, GROUP, d)
    u = alpha * x + (1.0 + gt_ref[...]) * y
    mu = jnp.mean(u, axis=-1, keepdims=True)
    uc = u - mu
    var = jnp.mean(uc * uc, axis=-1, keepdims=True)
    return uc * lax.rsqrt(var + LN_EPS) * g_ref[...] + b_ref[...]


def _ln_router_kernel(x_ref, y_ref, gt_ref, g_ref, b_ref, sc_ref, sh_ref, wr_ref, br_ref,
                      h_ref, hm_ref, lg_ref, *, alpha):
    gb = gt_ref.shape[0]
    d = x_ref.shape[-1]
    h = _deepnorm(x_ref, y_ref, gt_ref, g_ref, b_ref, alpha)
    hm = (h * (1.0 + sc_ref[...]) + sh_ref[...]).reshape(gb * GROUP, d)
    h_ref[...] = h.reshape(gb * GROUP, d)
    hm_ref[...] = hm
    lg_ref[...] = jnp.dot(hm, wr_ref[...], preferred_element_type=F32,
                          precision=lax.Precision.HIGHEST) + br_ref[...]


def ln_router(x, y, mod, gt_idx, sc_idx, sh_idx, g, b, w_router, b_router, layer, alpha):
    T, D = x.shape
    G = T // GROUP
    gb = _divisor(G, EW_GROUPS)
    E = w_router.shape[-1]
    rows = pl.BlockSpec((gb * GROUP, D), lambda i: (i, 0))
    vec = pl.BlockSpec((None, 1, D), lambda i: (layer, 0, 0))
    return pl.pallas_call(
        functools.partial(_ln_router_kernel, alpha=alpha),
        grid=(G // gb,),
        in_specs=[
            rows, rows,
            pl.BlockSpec((gb, 1, D), lambda i: (i, 0, gt_idx)),
            vec, vec,
            pl.BlockSpec((gb, 1, D), lambda i: (i, 0, sc_idx)),
            pl.BlockSpec((gb, 1, D), lambda i: (i, 0, sh_idx)),
            pl.BlockSpec((None, D, E), lambda i: (layer, 0, 0)),
            pl.BlockSpec((None, 1, E), lambda i: (layer, 0, 0)),
        ],
        out_specs=[rows, rows, pl.BlockSpec((gb * GROUP, E), lambda i: (i, 0))],
        out_shape=[
            jax.ShapeDtypeStruct((T, D), F32),
            jax.ShapeDtypeStruct((T, D), F32),
            jax.ShapeDtypeStruct((T, E), F32),
        ],
        compiler_params=_params(("arbitrary",)),
        name="ln_router",
    )(x, y, mod, g, b, mod, mod, w_router, b_router)


def _combine_ln_kernel(pos_ref, h_ref, yb_ref, gates_ref, gt_ref, g_ref, b_ref, o_ref, buf_ref, sem_ref, *, alpha):
    i = pl.program_id(0)
    n = pl.num_programs(0)
    tt = h_ref.shape[0]

    def row_copies(tile, slot, act):
        for j in range(TOP_K):
            def body(tok, c):
                p = pos_ref[(tile * tt + tok) * TOP_K + j]
                act(pltpu.make_async_copy(yb_ref.at[pl.ds(p, 1)],
                                          buf_ref.at[slot, pl.ds(j * tt + tok, 1)],
                                          sem_ref.at[slot]))
                return c
            lax.fori_loop(0, tt, body, 0, unroll=DMA_UNROLL)

    @pl.when(i == 0)
    def _():
        row_copies(0, 0, lambda cp: cp.start())

    @pl.when(i + 1 < n)
    def _():
        row_copies(i + 1, (i + 1) % 2, lambda cp: cp.start())

    slot = i % 2
    row_copies(i, slot, lambda cp: cp.wait())
    gates = gates_ref[...]
    y = buf_ref[slot, pl.ds(0, tt), :] * gates[:, 0:1]
    for j in range(1, TOP_K):
        y = y + buf_ref[slot, pl.ds(j * tt, tt), :] * gates[:, j:j + 1]
    gb = gt_ref.shape[0]
    d = h_ref.shape[-1]
    u = alpha * h_ref[...].reshape(gb, GROUP, d) + (1.0 + gt_ref[...]) * y.reshape(gb, GROUP, d)
    mu = jnp.mean(u, axis=-1, keepdims=True)
    uc = u - mu
    var = jnp.mean(uc * uc, axis=-1, keepdims=True)
    o_ref[...] = (uc * lax.rsqrt(var + LN_EPS) * g_ref[...] + b_ref[...]).reshape(gb * GROUP, d)


def combine_ln(h, yb, pos, gates, mod, gt_idx, g, b, layer, alpha):
    T, D = h.shape
    tt = COMBINE_TT
    gb = tt // GROUP
    assert T % tt == 0
    grid_spec = pltpu.PrefetchScalarGridSpec(
        num_scalar_prefetch=1,
        grid=(T // tt,),
        in_specs=[
            pl.BlockSpec((tt, D), lambda i, pos: (i, 0)),
            pl.BlockSpec(memory_space=pl.ANY),
            pl.BlockSpec((tt, TOP_K), lambda i, pos: (i, 0)),
            pl.BlockSpec((gb, 1, D), lambda i, pos: (i, 0, gt_idx)),
            pl.BlockSpec((None, 1, D), lambda i, pos: (layer, 0, 0)),
            pl.BlockSpec((None, 1, D), lambda i, pos: (layer, 0, 0)),
        ],
        out_specs=pl.BlockSpec((tt, D), lambda i, pos: (i, 0)),
        scratch_shapes=[pltpu.VMEM((2, TOP_K * tt, D), F32), pltpu.SemaphoreType.DMA((2,))],
    )
    return pl.pallas_call(
        functools.partial(_combine_ln_kernel, alpha=alpha),
        grid_spec=grid_spec,
        out_shape=jax.ShapeDtypeStruct((T, D), F32),
        compiler_params=_params(("arbitrary",)),
        name="combine_ln",
    )(pos, h, yb, gates, mod, g, b)


def _rwkv_kernel(a_ref, d_ref, b_ref, k_ref, r_ref, v_ref, s0_ref, y_ref, s_ref):
    @pl.when(pl.program_id(1) == 0)
    def _():
        s_ref[...] = s0_ref[...]

    n_steps = a_ref.shape[0]
    n_rows = v_ref.shape[1]

    def step(t, carry):
        a = a_ref[t]
        d = d_ref[t]
        b = b_ref[t]
        k = k_ref[t]
        r = r_ref[t]

        def row(v, c):
            s = s_ref[v]
            sa = jnp.sum(s * a, axis=0, keepdims=True)
            vv = v_ref[t, pl.ds(v, 1), :]
            s = s * d + sa * b + vv * k
            s_ref[v] = s
            y_ref[t, pl.ds(v, 1), :] = jnp.sum(s * r, axis=0, keepdims=True)
            return c

        return lax.fori_loop(0, n_rows, row, carry, unroll=True)

    lax.fori_loop(0, n_steps, step, 0)


def rwkv_scan(a, d, b, k, r, v, s0):
    L, K, NL = a.shape
    VR = v.shape[1]
    tb = SEQ_TB if L % SEQ_TB == 0 else L
    vec = pl.BlockSpec((tb, K, LANES), lambda g, t: (t, 0, g))
    val = pl.BlockSpec((tb, VR, LANES), lambda g, t: (t, 0, g))
    st = pl.BlockSpec((VR, K, LANES), lambda g, t: (0, 0, g))
    return pl.pallas_call(
        _rwkv_kernel,
        grid=(NL // LANES, L // tb),
        in_specs=[vec, vec, vec, vec, vec, val, st],
        out_specs=[val, st],
        out_shape=[jax.ShapeDtypeStruct((L, VR, NL), F32), jax.ShapeDtypeStruct((VR, K, NL), F32)],
        compiler_params=_params(("arbitrary", "arbitrary")),
        name="rwkv_scan",
    )(a, d, b, k, r, v, s0)


def to_lanes(x, split):
    B, L, H, C = x.shape
    xt = jnp.transpose(x, (1, 3, 0, 2))[..., None]
    return jnp.broadcast_to(xt, (L, C, B, H, split)).reshape(L, C, B * H * split)


def vals_to_lanes(v, split):
    B, L, H, C = v.shape
    vt = jnp.transpose(v.reshape(B, L, H, split, C // split), (1, 4, 0, 2, 3))
    return vt.reshape(L, C // split, B * H * split)


def vals_from_lanes(y, B, H, split):
    L, cs, _ = y.shape
    yt = jnp.transpose(y.reshape(L, cs, B, H, split), (2, 0, 3, 4, 1))
    return yt.reshape(B, L, H * split * cs)


def _hgrn_kernel(*refs, n_heads, has_state):
    if has_state:
        zq_ref, zf_ref, zi_ref, zg_ref, lb_ref, nw_ref, s0_ref, o_ref, s_ref, st_ref, oi_ref = refs
    else:
        zq_ref, zf_ref, zi_ref, zg_ref, lb_ref, nw_ref, o_ref, s_ref, st_ref, oi_ref = refs
    c = pl.program_id(1)
    last = pl.num_programs(1) - 1
    C = zq_ref.shape[0]
    dk = st_ref.shape[2]

    @pl.when(c == 0)
    def _():
        if has_state:
            for h in range(n_heads):
                st_ref[h] = s0_ref[h].T
        else:
            st_ref[...] = jnp.zeros_like(st_ref)

    row = lax.broadcasted_iota(jnp.int32, (C, C), 0)
    col = lax.broadcasted_iota(jnp.int32, (C, C), 1)
    tril = (row >= col).astype(F32)
    for h in range(n_heads):
        sl = slice(h * dk, (h + 1) * dk)
        zq, zf, v, zg = zq_ref[:, sl], zf_ref[:, sl], zi_ref[:, sl], zg_ref[:, sl]
        lb = lb_ref[:, sl]
        q = zq * jax.nn.sigmoid(zq)
        f = lb + (1.0 - lb) * jax.nn.sigmoid(zf)
        kf = (1.0 - lb) * jax.nn.sigmoid(-zf)
        g = jnp.log(jnp.maximum(f, F_FLOOR))
        b = jnp.dot(tril, g, preferred_element_type=F32, precision=lax.Precision.HIGHEST)
        st = st_ref[h]
        o = lax.dot_general(q * jnp.exp(b), st, (((1,), (1,)), ((), ())), preferred_element_type=F32)
        for t in range(C):
            n = min(C, -(-(t + 1) // SUBLANES) * SUBLANES)
            keep = lax.broadcasted_iota(jnp.int32, (n, 1), 0) <= t
            diff = b[t:t + 1, :] - b[:n]
            e = jnp.where(keep, jnp.exp(jnp.where(keep, diff, 0.0)), 0.0)
            a_t = jnp.sum(q[t:t + 1, :] * kf[:n] * e, axis=-1, keepdims=True)
            oi_ref[pl.ds(t, 1), :] = jnp.sum(a_t * v[:n], axis=0, keepdims=True)
        o = o + oi_ref[...]
        b_last = b[C - 1:C, :]
        kdec = kf * jnp.exp(b_last - b)
        st_ref[h] = st * jnp.exp(b_last) + jnp.dot(v.T, kdec, preferred_element_type=F32,
                                                   precision=lax.Precision.HIGHEST)
        o = o * lax.rsqrt(jnp.mean(o * o, axis=-1, keepdims=True) + LN_EPS) * nw_ref[...]
        o_ref[:, sl] = (o * (zg * jax.nn.sigmoid(zg))).astype(o_ref.dtype)

    @pl.when(c == last)
    def _():
        for h in range(n_heads):
            s_ref[h] = st_ref[h].T


def hgrn_mix(z, col0, lb, nw, layer, s0, B, L, row0, n_heads, dk, dv):
    assert dk == dv and dk % LANES == 0
    width = n_heads * dk
    C = _divisor(L, HG_CHUNK)
    nc = L // C
    assert col0 % width == 0 and row0 % C == 0
    cb, rb = col0 // width, row0 // C

    def zspec(k):
        return pl.BlockSpec((C, width), lambda b, c: (rb + b * nc + c, cb + k))

    in_specs = [zspec(0), zspec(1), zspec(2), zspec(3),
                pl.BlockSpec((None, 1, width), lambda b, c: (layer, 0, 0)),
                pl.BlockSpec((None, 1, dv), lambda b, c: (layer, 0, 0))]
    args = [z, z, z, z, lb, nw]
    st_spec = pl.BlockSpec((None, n_heads, dk, dv), lambda b, c: (b, 0, 0, 0))
    if s0 is not None:
        in_specs.append(st_spec)
        args.append(s0)
    return pl.pallas_call(
        functools.partial(_hgrn_kernel, n_heads=n_heads, has_state=s0 is not None),
        grid=(B, nc),
        in_specs=in_specs,
        out_specs=[pl.BlockSpec((C, width), lambda b, c: (b * nc + c, 0)), st_spec],
        out_shape=[jax.ShapeDtypeStruct((B * L, width), F32),
                   jax.ShapeDtypeStruct((B, n_heads, dk, dv), F32)],
        scratch_shapes=[pltpu.VMEM((n_heads, dv, dk), F32), pltpu.VMEM((C, dv), F32)],
        compiler_params=_params(("arbitrary", "arbitrary")),
        name="hgrn_mix",
    )(*args)


def _moe_kernel(ue_ref, ub_ref, ul_ref, nu_ref, tok_ref, hm_ref, wg_ref, wl_ref, bg_ref, bl_ref, wd_ref, bd_ref,
                yb_ref, xrow_ref, x_ref, acc_ref, gsem_ref, osem_ref):
    u = pl.program_id(0)
    j = pl.program_id(1)
    last = pl.num_programs(1) - 1
    tm = xrow_ref.shape[1]
    n_units, n_used = nu_ref[0], nu_ref[1]
    blk0, nblk = ub_ref[u], ul_ref[u]

    def row_copies(blk, act):
        slot = blk % 2

        def body(r, c):
            t = tok_ref[blk * tm + r]
            act(pltpu.make_async_copy(hm_ref.at[pl.ds(t, 1)], xrow_ref.at[slot, pl.ds(r, 1)], gsem_ref.at[slot]))
            return c
        lax.fori_loop(0, tm, body, 0, unroll=DMA_UNROLL)

    def out_copy(r):
        return pltpu.make_async_copy(acc_ref.at[pl.ds(r * tm, tm)], yb_ref.at[pl.ds((blk0 + r) * tm, tm)],
                                     osem_ref.at[0])

    @pl.when(jnp.logical_and(u == 0, j == 0))
    def _():
        row_copies(0, lambda cp: cp.start())

        @pl.when(n_used > 1)
        def _():
            row_copies(1, lambda cp: cp.start())

    @pl.when(u < n_units)
    def _():
        wg = wg_ref[...].astype(BF16)
        wl = wl_ref[...].astype(BF16)
        wd = wd_ref[...].astype(BF16)

        def fetch(r):
            b = blk0 + r
            row_copies(b, lambda cp: cp.wait())
            x_ref[pl.ds(pl.multiple_of(r * tm, tm), tm), :] = xrow_ref[b % 2].astype(BF16)

            @pl.when(b + 2 < n_used)
            def _():
                row_copies(b + 2, lambda cp: cp.start())

        def multiply(row0, m):
            rows = pl.ds(row0, m)
            x = x_ref[rows, :]
            g = jnp.dot(x, wg, preferred_element_type=F32) + bg_ref[...]
            lin = jnp.dot(x, wl, preferred_element_type=F32) + bl_ref[...]
            glu = jnp.minimum(g, SWIGLU_LIMIT)
            lin = jnp.clip(lin, -SWIGLU_LIMIT, SWIGLU_LIMIT)
            act = (lin + 1.0) * glu * jax.nn.sigmoid(SWIGLU_ALPHA * glu)
            part = jnp.dot(act.astype(BF16), wd, preferred_element_type=F32)

            @pl.when(j == 0)
            def _():
                acc_ref[rows, :] = part

            @pl.when(j > 0)
            def _():
                acc_ref[rows, :] += part

            @pl.when(j == last)
            def _():
                acc_ref[rows, :] += bd_ref[...]

        def pair(p, c):
            r0 = 2 * p

            @pl.when(j == 0)
            def _():
                fetch(r0)
                fetch(r0 + 1)

            multiply(pl.multiple_of(r0 * tm, 2 * tm), 2 * tm)

            @pl.when(j == last)
            def _():
                out_copy(r0).start()
                out_copy(r0 + 1).start()

            return c

        lax.fori_loop(0, nblk // 2, pair, 0)

        @pl.when(nblk % 2 == 1)
        def _():
            r0 = nblk - 1

            @pl.when(j == 0)
            def _():
                fetch(r0)

            multiply(pl.multiple_of(r0 * tm, tm), tm)

            @pl.when(j == last)
            def _():
                out_copy(r0).start()

        @pl.when(j == last)
        def _():
            def drain(r, c):
                out_copy(r).wait()
                return c
            lax.fori_loop(0, nblk, drain, 0)

    @pl.when(jnp.logical_and(u == pl.num_programs(0) - 1, j == last))
    def _():
        n_blocks = yb_ref.shape[0] // tm
        xrow_ref[0] = jnp.zeros(xrow_ref.shape[1:], F32)

        def fill(b):
            return pltpu.make_async_copy(xrow_ref.at[0], yb_ref.at[pl.ds(b * tm, tm)], osem_ref.at[0])

        def start(b, c):
            fill(b).start()
            return c

        def wait(b, c):
            fill(b).wait()
            return c

        lax.fori_loop(n_used, n_blocks, start, 0)
        lax.fori_loop(n_used, n_blocks, wait, 0)


def moe_experts(unit_e, unit_blk, unit_len, n_units_used, tok_pad, hm, w_gu, b_gu, w_down, b_down, layer):
    rows = tok_pad.shape[0]
    D = hm.shape[1]
    d_ff = w_down.shape[2]
    tm, tf = MOE_TM, MOE_TF
    nff = d_ff // tf

    def hold(u, j, nu):
        return jnp.where(u < nu[0], j, nff - 1)

    grid_spec = pltpu.PrefetchScalarGridSpec(
        num_scalar_prefetch=5,
        grid=(unit_e.shape[0], nff),
        in_specs=[
            pl.BlockSpec(memory_space=pl.ANY),
            pl.BlockSpec((None, None, D, tf), lambda u, j, ue, ub, ul, nu, tok: (layer, ue[u], 0, hold(u, j, nu))),
            pl.BlockSpec((None, None, D, tf), lambda u, j, ue, ub, ul, nu, tok: (layer, ue[u], 0, hold(u, j, nu) + nff)),
            pl.BlockSpec((None, None, 1, tf), lambda u, j, ue, ub, ul, nu, tok: (layer, ue[u], 0, hold(u, j, nu))),
            pl.BlockSpec((None, None, 1, tf), lambda u, j, ue, ub, ul, nu, tok: (layer, ue[u], 0, hold(u, j, nu) + nff)),
            pl.BlockSpec((None, None, tf, D), lambda u, j, ue, ub, ul, nu, tok: (layer, ue[u], hold(u, j, nu), 0)),
            pl.BlockSpec((None, None, 1, D), lambda u, j, ue, ub, ul, nu, tok: (layer, ue[u], 0, 0)),
        ],
        out_specs=pl.BlockSpec(memory_space=pl.ANY),
        scratch_shapes=[pltpu.VMEM((2, tm, D), F32), pltpu.VMEM((MOE_RUN * tm, D), BF16),
                        pltpu.VMEM((MOE_RUN * tm, D), F32),
                        pltpu.SemaphoreType.DMA((2,)), pltpu.SemaphoreType.DMA((1,))],
    )
    return pl.pallas_call(
        _moe_kernel,
        grid_spec=grid_spec,
        out_shape=jax.ShapeDtypeStruct((rows, D), F32),
        compiler_params=pltpu.CompilerParams(dimension_semantics=("arbitrary", "arbitrary"),
                                             vmem_limit_bytes=MOE_VMEM_LIMIT),
        name="moe_experts",
    )(unit_e, unit_blk, unit_len, n_units_used, tok_pad, hm, w_gu, w_gu, b_gu, b_gu, w_down, b_down)


def moe(hm, logits, w_gu, b_gu, w_down, b_down, layer):
    T, D = hm.shape
    E = logits.shape[-1]
    blk = MOE_TM
    top_val, top_idx = lax.top_k(logits, TOP_K)
    gates = jax.nn.softmax(top_val, axis=-1)
    TK = T * TOP_K
    flat_e = top_idx.reshape(TK).astype(jnp.int32)
    order = jnp.argsort(flat_e)
    e_sorted = flat_e[order]
    tok_sorted = (order // TOP_K).astype(jnp.int32)
    counts = jnp.bincount(flat_e, length=E).astype(jnp.int32)
    nblk_e = (counts + blk - 1) // blk
    blk_end = jnp.cumsum(nblk_e)
    blk_start = blk_end - nblk_e
    dest = blk_start[e_sorted] * blk + jnp.arange(TK, dtype=jnp.int32) - (jnp.cumsum(counts) - counts)[e_sorted]
    n_blocks = -(-(TK + E * (blk - 1)) // blk)
    rows = n_blocks * blk
    tok_pad = jnp.zeros((rows,), jnp.int32).at[dest].set(tok_sorted)
    nun_e = (nblk_e + MOE_RUN - 1) // MOE_RUN
    un_end = jnp.cumsum(nun_e)
    n_units = un_end[-1]
    max_units = -(-n_blocks // MOE_RUN) + E
    uid = jnp.minimum(jnp.arange(max_units, dtype=jnp.int32), n_units - 1)
    unit_e = jnp.minimum(jnp.searchsorted(un_end, uid, side='right'), E - 1).astype(jnp.int32)
    k_in_e = uid - (un_end - nun_e)[unit_e]
    unit_blk = (blk_start[unit_e] + k_in_e * MOE_RUN).astype(jnp.int32)
    unit_len = jnp.clip(nblk_e[unit_e] - k_in_e * MOE_RUN, 0, MOE_RUN)
    unit_len = jnp.where(jnp.arange(max_units) < n_units, unit_len, 0).astype(jnp.int32)
    n_used = jnp.stack([n_units, blk_end[-1]]).astype(jnp.int32)
    yb = moe_experts(unit_e, unit_blk, unit_len, n_used, tok_pad, hm, w_gu, b_gu, w_down, b_down, layer)
    pos = jnp.zeros((TK,), jnp.int32).at[order].set(dest)
    return yb, pos, gates


def kernel(x_prompt, x_sample, c_prompt, c_sample, state_rwkv, state_hgrn, state_shift, w_ada, b_ada, w_in, w_out, mu_rkv, mu_in, w0, w1, w2, a0, a1, a2, v0, v1, v2, mu_vg, g1, g2, k_k, k_a, r_k, lnx_g, lnx_b, hg_lower, hg_norm_w, ln1_g, ln1_b, ln2_g, ln2_b, w_router, b_router, w_gu, b_gu, w_down, b_down):
    Bp, Lp, D = x_prompt.shape
    Bs, Ls, _ = x_sample.shape
    depth = w_in.shape[0]
    d_a = w0.shape[1]
    rw_heads, rw_head = r_k.shape[1], r_k.shape[2]
    hg_dv = hg_norm_w.shape[1]
    nk = hg_lower.shape[1]
    d_b = D - d_a
    hg_heads = d_b // hg_dv
    hg_dk = nk // hg_heads
    nv = hg_heads * hg_dv
    n_exp = w_router.shape[-1]
    d_ff = w_down.shape[2]
    assert Lp % GROUP == 0 and Ls % GROUP == 0
    Tp, Ts = Bp * Lp, Bs * Ls
    T = Tp + Ts
    dn_alpha = (2 * depth) ** 0.25

    def split_groups(z):
        return z[:Tp].reshape(Bp, Lp, -1), z[Tp:].reshape(Bs, Ls, -1)

    def shift_tokens(z, prev_s):
        zp, zs = split_groups(z)
        zp = jnp.concatenate([jnp.zeros_like(zp[:, :1]), zp[:, :-1]], axis=1)
        zs = jnp.concatenate([prev_s[:, None], zs[:, :-1]], axis=1)
        return jnp.concatenate([zp.reshape(Tp, -1), zs.reshape(Ts, -1)])

    c_all = jax.nn.silu(jnp.concatenate([c_prompt, c_sample]))
    n_c = Bp + Bs
    c_pad = jnp.pad(c_all, ((0, (-n_c) % SUBLANES), (0, 0)))
    b_ada3 = b_ada[:, None, :]
    rep = jnp.concatenate([jnp.repeat(jnp.arange(Bp), Lp // GROUP), Bp + jnp.repeat(jnp.arange(Bs), Ls // GROUP)])

    p_lb = jax.nn.softmax(hg_lower.astype(F32), axis=0)
    lb_all3 = (jnp.cumsum(p_lb, axis=0) - p_lb[0])[:, None, :]
    hg_norm_w3 = hg_norm_w[:, None, :]
    assert nk == nv

    x = jnp.concatenate([x_prompt.reshape(Tp, D), x_sample.reshape(Ts, D)])
    ln1_g3, ln1_b3, ln2_g3, ln2_b3 = (t[:, None, :] for t in (ln1_g, ln1_b, ln2_g, ln2_b))
    b_router3 = b_router[:, None, :]
    b_gu4 = b_gu[:, :, None, :]
    b_down4 = b_down[:, :, None, :]

    v_first = None
    rw_p, rw_s, hg_p, hg_s, sh_p, sh_s = [], [], [], [], [], []
    for l in range(depth):
        mod_c = mm(c_pad, w_ada, l, b_ada3, name="ada")[:n_c]
        mod = mod_c[rep][:, None, :]
        SH1, SC1, GT1, SH2, SC2, GT2 = range(6)

        xm = modulate(x, mod, SC1, SH1)
        prev = state_shift[l]
        prev_bf = prev.astype(BF16)

        def last_rows(xg, m):
            sh1, sc1 = m[:, SH1 * D:(SH1 + 1) * D], m[:, SC1 * D:(SC1 + 1) * D]
            return xg[:, -1] * (1 + sc1) + sh1
        xp3, xs3 = split_groups(x)
        sh_p.append(last_rows(xp3, mod_c[:Bp]))
        sh_s.append(last_rows(xs3, mod_c[Bp:]))

        z = mm(xm, w_in, l, name="w_in")
        z_prev = mm(prev_bf, w_in, l, n_cols=3 * d_a, name="w_in_prev")

        lora_w = [w1[l], a1[l], g1[l]] + ([v1[l - 1]] if l > 0 else [])
        lora_mu = [mu_in[l, 0], mu_in[l, 1], mu_in[l, 2]] + ([mu_vg[l - 1]] if l > 0 else [])
        wl = jnp.concatenate(lora_w + [m[:, None] * w for m, w in zip(lora_mu, lora_w)], axis=1)
        nl = wl.shape[1] // 2
        zl = mm(xm, wl, name="lora_in")
        zl_prev = mm(prev_bf, wl, name="lora_in_prev")
        lora = zl[:, :nl] + shift_tokens(zl[:, nl:], zl_prev[:, nl:]) - zl[:, nl:]
        offs = [0]
        for w in lora_w:
            offs.append(offs[-1] + w.shape[1])
        lw, la, lg = (lora[:, offs[i]:offs[i + 1]] for i in range(3))

        zA = z[:, :3 * d_a]
        zA = zA + (shift_tokens(zA, z_prev) - zA) * mu_rkv[l]
        r, k, v = jnp.split(zA, 3, axis=-1)
        w_log = -jax.nn.softplus(-(w0[l] + mm(jnp.tanh(lw), w2, l, name="lora_w"))) - 0.5
        if l == 0:
            v_first = v
        else:
            lv = lora[:, offs[3]:offs[4]]
            v = v + (v_first - v) * jax.nn.sigmoid(v0[l - 1] + mm(lv, v2, l - 1, name="lora_v"))
        a = jax.nn.sigmoid(a0[l] + mm(la, a2, l, name="lora_a"))
        g = mm(jax.nn.sigmoid(lg), g2, l, name="lora_g")
        heads = lambda t: t.reshape(T, rw_heads, rw_head)
        kk = heads(k * k_k[l])
        kk = kk / jnp.maximum(jnp.sqrt(jnp.sum(kk * kk, -1, keepdims=True)), 1e-12)
        k = k * (1 + (a - 1) * k_a[l])
        decay = jnp.exp(-jnp.exp(w_log))
        a_vec = (-kk).reshape(T, d_a)
        b_vec = (kk * heads(a)).reshape(T, d_a)

        y_groups, s_groups = [], []
        for gi, (B, L, s0) in enumerate(((Bp, Lp, None), (Bs, Ls, state_rwkv[l]))):
            pick = lambda t: split_groups(t)[gi].reshape(B, L, rw_heads, rw_head)
            ins = [to_lanes(pick(t), RW_SPLIT) for t in (a_vec, decay, b_vec, k, r)]
            vl = vals_to_lanes(pick(v), RW_SPLIT)
            vr = rw_head // RW_SPLIT
            nlanes = B * rw_heads * RW_SPLIT
            if s0 is None:
                s0l = jnp.zeros((vr, rw_head, nlanes), F32)
            else:
                s0l = jnp.transpose(s0.reshape(B, rw_heads, RW_SPLIT, vr, rw_head), (3, 4, 0, 1, 2)).reshape(vr, rw_head, nlanes)
            yl, sl = rwkv_scan(*ins, vl, s0l)
            y_groups.append(vals_from_lanes(yl, B, rw_heads, RW_SPLIT).reshape(B * L, d_a))
            s_groups.append(jnp.transpose(sl.reshape(vr, rw_head, B, rw_heads, RW_SPLIT), (2, 3, 4, 0, 1)).reshape(B, rw_heads, rw_head, rw_head))
        rw_p.append(s_groups[0])
        rw_s.append(s_groups[1])
        y = heads(jnp.concatenate(y_groups))
        mu_y = jnp.mean(y, -1, keepdims=True)
        var_y = jnp.mean(jnp.square(y - mu_y), -1, keepdims=True)
        y = ((y - mu_y) * lax.rsqrt(var_y + RW_LN_EPS)).reshape(T, d_a) * lnx_g[l] + lnx_b[l]
        bonus = (jnp.sum(heads(r) * heads(k) * r_k[l], -1, keepdims=True) * heads(v)).reshape(T, d_a)
        oA = (y + bonus) * g

        oB_p, s_p = hgrn_mix(z, 3 * d_a, lb_all3, hg_norm_w3, l, None, Bp, Lp, 0, hg_heads, hg_dk, hg_dv)
        oB_s, s_s = hgrn_mix(z, 3 * d_a, lb_all3, hg_norm_w3, l, state_hgrn[l], Bs, Ls, Tp, hg_heads, hg_dk, hg_dv)
        hg_p.append(s_p)
        hg_s.append(s_s)
        oB = jnp.concatenate([oB_p, oB_s])

        mix = mm(jnp.concatenate([oA, oB], axis=-1).astype(BF16), w_out, l, name="w_out")
        h, hm, logits = ln_router(x, mix, mod, GT1, SC2, SH2, ln1_g3, ln1_b3, w_router, b_router3, l, dn_alpha)
        yb, pos, gates = moe(hm, logits, w_gu, b_gu4, w_down, b_down4, l)
        x = combine_ln(h, yb, pos, gates, mod, GT2, ln2_g3, ln2_b3, l, dn_alpha)

    y_prompt = x[:Tp].reshape(Bp, Lp, D)
    y_sample = x[Tp:].reshape(Bs, Ls, D)
    return (y_prompt, y_sample, jnp.stack(rw_p), jnp.stack(rw_s), jnp.stack(hg_p), jnp.stack(hg_s),
            jnp.stack(sh_p), jnp.stack(sh_s))
```

```python
import functools

import jax
import jax.numpy as jnp
from jax import lax
from jax.experimental import pallas as pl
from jax.experimental.pallas import tpu as pltpu

F32 = jnp.float32
BF16 = jnp.bfloat16

TOP_K = 4
SWIGLU_LIMIT = 7.0
SWIGLU_ALPHA = 1.702
LN_EPS = 1e-5
RW_LN_EPS = 64e-5
F_FLOOR = 1e-30

LANES = 128
SUBLANES = 8
GROUP = SUBLANES
VMEM_LIMIT = 56 * 1024 * 1024

MM_TM = 1024
MM_TN = 1024
EW_GROUPS = 64
MOE_TM = 256
MOE_TF = 256
MOE_RUN = 8
MOE_VMEM_LIMIT = 60 * 1024 * 1024
COMBINE_TT = 64
DMA_UNROLL = 8
HG_CHUNK = 32
SEQ_TB = 32


def _divisor(n, pref):
    d = min(n, pref)
    while n % d:
        d -= 1
    return d


def _params(sem):
    return pltpu.CompilerParams(dimension_semantics=sem, vmem_limit_bytes=VMEM_LIMIT)


def _mm_kernel(x_ref, w_ref, o_ref, wbf_ref):
    @pl.when(pl.program_id(1) == 0)
    def _():
        wbf_ref[...] = w_ref[...].astype(BF16)

    o_ref[...] = jnp.dot(x_ref[...].astype(BF16), wbf_ref[...], preferred_element_type=F32)


def _mm_bias_kernel(x_ref, w_ref, b_ref, o_ref, wbf_ref):
    @pl.when(pl.program_id(1) == 0)
    def _():
        wbf_ref[...] = w_ref[...].astype(BF16)

    o_ref[...] = jnp.dot(x_ref[...].astype(BF16), wbf_ref[...], preferred_element_type=F32) + b_ref[...]


def mm(x, w, layer=None, bias=None, *, n_cols=None, name="mm"):
    M, K = x.shape
    N = w.shape[-1] if n_cols is None else n_cols
    tm = MM_TM if M % MM_TM == 0 else M
    tn = MM_TN if N % MM_TN == 0 else N
    grid = (N // tn, M // tm)
    if layer is None:
        w_spec = pl.BlockSpec((K, tn), lambda j, i: (0, j))
    else:
        w_spec = pl.BlockSpec((None, K, tn), lambda j, i: (layer, 0, j))
    in_specs = [pl.BlockSpec((tm, K), lambda j, i: (i, 0)), w_spec]
    args = [x, w]
    kern = _mm_kernel
    if bias is not None:
        if layer is None:
            in_specs.append(pl.BlockSpec((1, tn), lambda j, i: (0, j)))
        else:
            in_specs.append(pl.BlockSpec((None, 1, tn), lambda j, i: (layer, 0, j)))
        args.append(bias)
        kern = _mm_bias_kernel
    return pl.pallas_call(
        kern,
        grid=grid,
        in_specs=in_specs,
        out_specs=pl.BlockSpec((tm, tn), lambda j, i: (i, j)),
        out_shape=jax.ShapeDtypeStruct((M, N), F32),
        scratch_shapes=[pltpu.VMEM((K, tn), BF16)],
        compiler_params=_params(("arbitrary", "arbitrary")),
        name=name,
    )(*args)


def _modulate_kernel(x_ref, sc_ref, sh_ref, o_ref):
    gb = sc_ref.shape[0]
    d = x_ref.shape[-1]
    x = x_ref[...].reshape(gb, GROUP, d)
    xm = x * (1.0 + sc_ref[...]) + sh_ref[...]
    o_ref[...] = xm.reshape(gb * GROUP, d).astype(o_ref.dtype)


def modulate(x, mod, sc_idx, sh_idx):
    T, D = x.shape
    G = T // GROUP
    gb = _divisor(G, EW_GROUPS)
    return pl.pallas_call(
        _modulate_kernel,
        grid=(G // gb,),
        in_specs=[
            pl.BlockSpec((gb * GROUP, D), lambda i: (i, 0)),
            pl.BlockSpec((gb, 1, D), lambda i: (i, 0, sc_idx)),
            pl.BlockSpec((gb, 1, D), lambda i: (i, 0, sh_idx)),
        ],
        out_specs=pl.BlockSpec((gb * GROUP, D), lambda i: (i, 0)),
        out_shape=jax.ShapeDtypeStruct((T, D), BF16),
        compiler_params=_params(("arbitrary",)),
        name="modulate",
    )(x, mod, mod)


def _deepnorm(x_ref, y_ref, gt_ref, g_ref, b_ref, alpha):
    gb = gt_ref.shape[0]
    d = x_ref.shape[-1]
    x = x_ref[...].reshape(gb, GROUP, d)
    y = y_ref[...].reshape(gb, GROUP, d)
    u = alpha * x + (1.0 + gt_ref[...]) * y
    mu = jnp.mean(u, axis=-1, keepdims=True)
    uc = u - mu
    var = jnp.mean(uc * uc, axis=-1, keepdims=True)
    return uc * lax.rsqrt(var + LN_EPS) * g_ref[...] + b_ref[...]


def _ln_router_kernel(x_ref, y_ref, gt_ref, g_ref, b_ref, sc_ref, sh_ref, wr_ref, br_ref,
                      h_ref, hm_ref, lg_ref, *, alpha):
    gb = gt_ref.shape[0]
    d = x_ref.shape[-1]
    h = _deepnorm(x_ref, y_ref, gt_ref, g_ref, b_ref, alpha)
    hm = (h * (1.0 + sc_ref[...]) + sh_ref[...]).reshape(gb * GROUP, d)
    h_ref[...] = h.reshape(gb * GROUP, d)
    hm_ref[...] = hm
    lg_ref[...] = jnp.dot(hm.astype(BF16), wr_ref[...].astype(BF16), preferred_element_type=F32) + br_ref[...]


def ln_router(x, y, mod, gt_idx, sc_idx, sh_idx, g, b, w_router, b_router, layer, alpha):
    T, D = x.shape
    G = T // GROUP
    gb = _divisor(G, EW_GROUPS)
    E = w_router.shape[-1]
    rows = pl.BlockSpec((gb * GROUP, D), lambda i: (i, 0))
    vec = pl.BlockSpec((None, 1, D), lambda i: (layer, 0, 0))
    return pl.pallas_call(
        functools.partial(_ln_router_kernel, alpha=alpha),
        grid=(G // gb,),
        in_specs=[
            rows, rows,
            pl.BlockSpec((gb, 1, D), lambda i: (i, 0, gt_idx)),
            vec, vec,
            pl.BlockSpec((gb, 1, D), lambda i: (i, 0, sc_idx)),
            pl.BlockSpec((gb, 1, D), lambda i: (i, 0, sh_idx)),
            pl.BlockSpec((None, D, E), lambda i: (layer, 0, 0)),
            pl.BlockSpec((None, 1, E), lambda i: (layer, 0, 0)),
        ],
        out_specs=[rows, rows, pl.BlockSpec((gb * GROUP, E), lambda i: (i, 0))],
        out_shape=[
            jax.ShapeDtypeStruct((T, D), F32),
            jax.ShapeDtypeStruct((T, D), F32),
            jax.ShapeDtypeStruct((T, E), F32),
        ],
        compiler_params=_params(("arbitrary",)),
        name="ln_router",
    )(x, y, mod, g, b, mod, mod, w_router, b_router)


def _combine_ln_kernel(pos_ref, h_ref, yb_ref, gates_ref, gt_ref, g_ref, b_ref, o_ref, buf_ref, sem_ref, *, alpha):
    i = pl.program_id(0)
    n = pl.num_programs(0)
    tt = h_ref.shape[0]

    def row_copies(tile, slot, act):
        for j in range(TOP_K):
            def body(tok, c):
                p = pos_ref[(tile * tt + tok) * TOP_K + j]
                act(pltpu.make_async_copy(yb_ref.at[pl.ds(p, 1)],
                                          buf_ref.at[slot, pl.ds(j * tt + tok, 1)],
                                          sem_ref.at[slot]))
                return c
            lax.fori_loop(0, tt, body, 0, unroll=DMA_UNROLL)

    @pl.when(i == 0)
    def _():
        row_copies(0, 0, lambda cp: cp.start())

    @pl.when(i + 1 < n)
    def _():
        row_copies(i + 1, (i + 1) % 2, lambda cp: cp.start())

    slot = i % 2
    row_copies(i, slot, lambda cp: cp.wait())
    gates = gates_ref[...]
    y = buf_ref[slot, pl.ds(0, tt), :] * gates[:, 0:1]
    for j in range(1, TOP_K):
        y = y + buf_ref[slot, pl.ds(j * tt, tt), :] * gates[:, j:j + 1]
    gb = gt_ref.shape[0]
    d = h_ref.shape[-1]
    u = alpha * h_ref[...].reshape(gb, GROUP, d) + (1.0 + gt_ref[...]) * y.reshape(gb, GROUP, d)
    mu = jnp.mean(u, axis=-1, keepdims=True)
    uc = u - mu
    var = jnp.mean(uc * uc, axis=-1, keepdims=True)
    o_ref[...] = (uc * lax.rsqrt(var + LN_EPS) * g_ref[...] + b_ref[...]).reshape(gb * GROUP, d)


def combine_ln(h, yb, pos, gates, mod, gt_idx, g, b, layer, alpha):
    T, D = h.shape
    tt = COMBINE_TT
    gb = tt // GROUP
    assert T % tt == 0
    grid_spec = pltpu.PrefetchScalarGridSpec(
        num_scalar_prefetch=1,
        grid=(T // tt,),
        in_specs=[
            pl.BlockSpec((tt, D), lambda i, pos: (i, 0)),
            pl.BlockSpec(memory_space=pl.ANY),
            pl.BlockSpec((tt, TOP_K), lambda i, pos: (i, 0)),
            pl.BlockSpec((gb, 1, D), lambda i, pos: (i, 0, gt_idx)),
            pl.BlockSpec((None, 1, D), lambda i, pos: (layer, 0, 0)),
            pl.BlockSpec((None, 1, D), lambda i, pos: (layer, 0, 0)),
        ],
        out_specs=pl.BlockSpec((tt, D), lambda i, pos: (i, 0)),
        scratch_shapes=[pltpu.VMEM((2, TOP_K * tt, D), F32), pltpu.SemaphoreType.DMA((2,))],
    )
    return pl.pallas_call(
        functools.partial(_combine_ln_kernel, alpha=alpha),
        grid_spec=grid_spec,
        out_shape=jax.ShapeDtypeStruct((T, D), F32),
        compiler_params=_params(("arbitrary",)),
        name="combine_ln",
    )(pos, h, yb, gates, mod, g, b)


def _rwkv_kernel(a_ref, d_ref, b_ref, k_ref, r_ref, v_ref, s0_ref, y_ref, s_ref, wide_ref):
    @pl.when(pl.program_id(1) == 0)
    def _():
        s_ref[...] = s0_ref[...]

    n_steps = a_ref.shape[0]
    n_rows = v_ref.shape[1]
    copies = LANES // a_ref.shape[2]
    key_refs = (a_ref, d_ref, b_ref, k_ref, r_ref)

    if copies > 1:
        def widen(t, carry):
            for i, ref in enumerate(key_refs):
                wide_ref[t, i] = jnp.concatenate([ref[t]] * copies, axis=-1)
            return carry
        lax.fori_loop(0, n_steps, widen, 0)

    def key_side(i, t):
        return wide_ref[t, i] if copies > 1 else key_refs[i][t]

    def step(t, carry):
        a = key_side(0, t)
        d = key_side(1, t)
        b = key_side(2, t)
        k = key_side(3, t)
        r = key_side(4, t)

        def row(v, c):
            s = s_ref[v]
            sa = jnp.sum(s * a, axis=0, keepdims=True)
            vv = v_ref[t, pl.ds(v, 1), :]
            s = s * d + sa * b + vv * k
            s_ref[v] = s
            y_ref[t, pl.ds(v, 1), :] = jnp.sum(s * r, axis=0, keepdims=True)
            return c

        return lax.fori_loop(0, n_rows, row, carry, unroll=True)

    lax.fori_loop(0, n_steps, step, 0)


def rwkv_scan(a, d, b, k, r, v, s0):
    L, K, NC = a.shape
    VR, NL = v.shape[1], v.shape[2]
    split = NL // NC
    assert NL % LANES == 0 and (split == 1 or NL == LANES)
    tb = SEQ_TB if L % SEQ_TB == 0 else L
    vec = pl.BlockSpec((tb, K, LANES // split), lambda g, t: (t, 0, g))
    val = pl.BlockSpec((tb, VR, LANES), lambda g, t: (t, 0, g))
    st = pl.BlockSpec((VR, K, LANES), lambda g, t: (0, 0, g))
    return pl.pallas_call(
        _rwkv_kernel,
        grid=(NL // LANES, L // tb),
        in_specs=[vec, vec, vec, vec, vec, val, st],
        out_specs=[val, st],
        out_shape=[jax.ShapeDtypeStruct((L, VR, NL), F32), jax.ShapeDtypeStruct((VR, K, NL), F32)],
        scratch_shapes=[pltpu.VMEM((tb if split > 1 else 1, 5, K, LANES), F32)],
        compiler_params=_params(("arbitrary", "arbitrary")),
        name="rwkv_scan",
    )(a, d, b, k, r, v, s0)


def chains_to_lanes(x):
    B, L, H, C = x.shape
    return jnp.transpose(x, (1, 3, 0, 2)).reshape(L, C, B * H)


def vals_to_lanes(v, split):
    B, L, H, C = v.shape
    vt = jnp.transpose(v.reshape(B, L, H, split, C // split), (1, 4, 3, 0, 2))
    return vt.reshape(L, C // split, split * B * H)


def vals_from_lanes(y, B, H, split):
    L, cs, _ = y.shape
    yt = jnp.transpose(y.reshape(L, cs, split, B, H), (3, 0, 4, 2, 1))
    return yt.reshape(B, L, H * split * cs)


def _hgrn_kernel(*refs, n_heads, has_state):
    if has_state:
        zq_ref, zf_ref, zi_ref, zg_ref, lb_ref, nw_ref, s0_ref, o_ref, s_ref, st_ref, oi_ref = refs
    else:
        zq_ref, zf_ref, zi_ref, zg_ref, lb_ref, nw_ref, o_ref, s_ref, st_ref, oi_ref = refs
    c = pl.program_id(1)
    last = pl.num_programs(1) - 1
    C = zq_ref.shape[0]
    dk = st_ref.shape[2]

    @pl.when(c == 0)
    def _():
        if has_state:
            for h in range(n_heads):
                st_ref[h] = s0_ref[h].T
        else:
            st_ref[...] = jnp.zeros_like(st_ref)

    row = lax.broadcasted_iota(jnp.int32, (C, C), 0)
    col = lax.broadcasted_iota(jnp.int32, (C, C), 1)
    tril = (row >= col).astype(F32)
    for h in range(n_heads):
        sl = slice(h * dk, (h + 1) * dk)
        zq, zf, v, zg = zq_ref[:, sl], zf_ref[:, sl], zi_ref[:, sl], zg_ref[:, sl]
        lb = lb_ref[:, sl]
        q = zq * jax.nn.sigmoid(zq)
        f = lb + (1.0 - lb) * jax.nn.sigmoid(zf)
        kf = (1.0 - lb) * jax.nn.sigmoid(-zf)
        g = jnp.log(jnp.maximum(f, F_FLOOR))
        b = jnp.dot(tril, g, preferred_element_type=F32, precision=lax.Precision.HIGHEST)
        st = st_ref[h]
        o = lax.dot_general(q * jnp.exp(b), st, (((1,), (1,)), ((), ())), preferred_element_type=F32)
        for t in range(C):
            n = min(C, -(-(t + 1) // SUBLANES) * SUBLANES)
            keep = lax.broadcasted_iota(jnp.int32, (n, 1), 0) <= t
            diff = b[t:t + 1, :] - b[:n]
            e = jnp.where(keep, jnp.exp(jnp.where(keep, diff, 0.0)), 0.0)
            a_t = jnp.sum(q[t:t + 1, :] * kf[:n] * e, axis=-1, keepdims=True)
            oi_ref[pl.ds(t, 1), :] = jnp.sum(a_t * v[:n], axis=0, keepdims=True)
        o = o + oi_ref[...]
        b_last = b[C - 1:C, :]
        kdec = kf * jnp.exp(b_last - b)
        st_ref[h] = st * jnp.exp(b_last) + jnp.dot(v.T, kdec, preferred_element_type=F32,
                                                   precision=lax.Precision.HIGHEST)
        o = o * lax.rsqrt(jnp.mean(o * o, axis=-1, keepdims=True) + LN_EPS) * nw_ref[...]
        o_ref[:, sl] = (o * (zg * jax.nn.sigmoid(zg))).astype(o_ref.dtype)

    @pl.when(c == last)
    def _():
        for h in range(n_heads):
            s_ref[h] = st_ref[h].T


def hgrn_mix(z, col0, lb, nw, layer, s0, B, L, row0, n_heads, dk, dv):
    assert dk == dv and dk % LANES == 0
    width = n_heads * dk
    C = _divisor(L, HG_CHUNK)
    nc = L // C
    assert col0 % width == 0 and row0 % C == 0
    cb, rb = col0 // width, row0 // C

    def zspec(k):
        return pl.BlockSpec((C, width), lambda b, c: (rb + b * nc + c, cb + k))

    in_specs = [zspec(0), zspec(1), zspec(2), zspec(3),
                pl.BlockSpec((None, 1, width), lambda b, c: (layer, 0, 0)),
                pl.BlockSpec((None, 1, dv), lambda b, c: (layer, 0, 0))]
    args = [z, z, z, z, lb, nw]
    st_spec = pl.BlockSpec((None, n_heads, dk, dv), lambda b, c: (b, 0, 0, 0))
    if s0 is not None:
        in_specs.append(st_spec)
        args.append(s0)
    return pl.pallas_call(
        functools.partial(_hgrn_kernel, n_heads=n_heads, has_state=s0 is not None),
        grid=(B, nc),
        in_specs=in_specs,
        out_specs=[pl.BlockSpec((C, width), lambda b, c: (b * nc + c, 0)), st_spec],
        out_shape=[jax.ShapeDtypeStruct((B * L, width), F32),
                   jax.ShapeDtypeStruct((B, n_heads, dk, dv), F32)],
        scratch_shapes=[pltpu.VMEM((n_heads, dv, dk), F32), pltpu.VMEM((C, dv), F32)],
        compiler_params=_params(("arbitrary", "arbitrary")),
        name="hgrn_mix",
    )(*args)


def _moe_kernel(ue_ref, ub_ref, ul_ref, nu_ref, tok_ref, hm_ref, wg_ref, wl_ref, bg_ref, bl_ref, wd_ref, bd_ref,
                yb_ref, xrow_ref, x_ref, acc_ref, gsem_ref, osem_ref):
    u = pl.program_id(0)
    j = pl.program_id(1)
    last = pl.num_programs(1) - 1
    tm = xrow_ref.shape[1]
    n_units, n_used = nu_ref[0], nu_ref[1]
    blk0, nblk = ub_ref[u], ul_ref[u]

    def row_copies(blk, act):
        slot = blk % 2

        def body(r, c):
            t = tok_ref[blk * tm + r]
            act(pltpu.make_async_copy(hm_ref.at[pl.ds(t, 1)], xrow_ref.at[slot, pl.ds(r, 1)], gsem_ref.at[slot]))
            return c
        lax.fori_loop(0, tm, body, 0, unroll=DMA_UNROLL)

    def out_copy(r):
        return pltpu.make_async_copy(acc_ref.at[pl.ds(r * tm, tm)], yb_ref.at[pl.ds((blk0 + r) * tm, tm)],
                                     osem_ref.at[0])

    @pl.when(jnp.logical_and(u == 0, j == 0))
    def _():
        row_copies(0, lambda cp: cp.start())

        @pl.when(n_used > 1)
        def _():
            row_copies(1, lambda cp: cp.start())

    @pl.when(u < n_units)
    def _():
        wg = wg_ref[...].astype(BF16)
        wl = wl_ref[...].astype(BF16)
        wd = wd_ref[...].astype(BF16)

        def fetch(r):
            b = blk0 + r
            row_copies(b, lambda cp: cp.wait())
            x_ref[pl.ds(pl.multiple_of(r * tm, tm), tm), :] = xrow_ref[b % 2].astype(BF16)

            @pl.when(b + 2 < n_used)
            def _():
                row_copies(b + 2, lambda cp: cp.start())

        def multiply(row0, m):
            rows = pl.ds(row0, m)
            x = x_ref[rows, :]
            g = jnp.dot(x, wg, preferred_element_type=F32) + bg_ref[...]
            lin = jnp.dot(x, wl, preferred_element_type=F32) + bl_ref[...]
            glu = jnp.minimum(g, SWIGLU_LIMIT)
            lin = jnp.clip(lin, -SWIGLU_LIMIT, SWIGLU_LIMIT)
            act = (lin + 1.0) * glu * jax.nn.sigmoid(SWIGLU_ALPHA * glu)
            part = jnp.dot(act.astype(BF16), wd, preferred_element_type=F32)

            @pl.when(j == 0)
            def _():
                acc_ref[rows, :] = part

            @pl.when(j > 0)
            def _():
                acc_ref[rows, :] += part

            @pl.when(j == last)
            def _():
                acc_ref[rows, :] += bd_ref[...]

        def pair(p, c):
            r0 = 2 * p

            @pl.when(j == 0)
            def _():
                fetch(r0)
                fetch(r0 + 1)

            multiply(pl.multiple_of(r0 * tm, 2 * tm), 2 * tm)

            @pl.when(j == last)
            def _():
                out_copy(r0).start()
                out_copy(r0 + 1).start()

            return c

        lax.fori_loop(0, nblk // 2, pair, 0)

        @pl.when(nblk % 2 == 1)
        def _():
            r0 = nblk - 1

            @pl.when(j == 0)
            def _():
                fetch(r0)

            multiply(pl.multiple_of(r0 * tm, tm), tm)

            @pl.when(j == last)
            def _():
                out_copy(r0).start()

        @pl.when(j == last)
        def _():
            def drain(r, c):
                out_copy(r).wait()
                return c
            lax.fori_loop(0, nblk, drain, 0)

    @pl.when(jnp.logical_and(u == pl.num_programs(0) - 1, j == last))
    def _():
        n_blocks = yb_ref.shape[0] // tm
        xrow_ref[0] = jnp.zeros(xrow_ref.shape[1:], F32)

        def fill(b):
            return pltpu.make_async_copy(xrow_ref.at[0], yb_ref.at[pl.ds(b * tm, tm)], osem_ref.at[0])

        def start(b, c):
            fill(b).start()
            return c

        def wait(b, c):
            fill(b).wait()
            return c

        lax.fori_loop(n_used, n_blocks, start, 0)
        lax.fori_loop(n_used, n_blocks, wait, 0)


def moe_experts(unit_e, unit_blk, unit_len, n_units_used, tok_pad, hm, w_gu, b_gu, w_down, b_down, layer):
    rows = tok_pad.shape[0]
    D = hm.shape[1]
    d_ff = w_down.shape[2]
    tm, tf = MOE_TM, MOE_TF
    nff = d_ff // tf

    def hold(u, j, nu):
        return jnp.where(u < nu[0], j, nff - 1)

    grid_spec = pltpu.PrefetchScalarGridSpec(
        num_scalar_prefetch=5,
        grid=(unit_e.shape[0], nff),
        in_specs=[
            pl.BlockSpec(memory_space=pl.ANY),
            pl.BlockSpec((None, None, D, tf), lambda u, j, ue, ub, ul, nu, tok: (layer, ue[u], 0, hold(u, j, nu))),
            pl.BlockSpec((None, None, D, tf), lambda u, j, ue, ub, ul, nu, tok: (layer, ue[u], 0, hold(u, j, nu) + nff)),
            pl.BlockSpec((None, None, 1, tf), lambda u, j, ue, ub, ul, nu, tok: (layer, ue[u], 0, hold(u, j, nu))),
            pl.BlockSpec((None, None, 1, tf), lambda u, j, ue, ub, ul, nu, tok: (layer, ue[u], 0, hold(u, j, nu) + nff)),
            pl.BlockSpec((None, None, tf, D), lambda u, j, ue, ub, ul, nu, tok: (layer, ue[u], hold(u, j, nu), 0)),
            pl.BlockSpec((None, None, 1, D), lambda u, j, ue, ub, ul, nu, tok: (layer, ue[u], 0, 0)),
        ],
        out_specs=pl.BlockSpec(memory_space=pl.ANY),
        scratch_shapes=[pltpu.VMEM((2, tm, D), F32), pltpu.VMEM((MOE_RUN * tm, D), BF16),
                        pltpu.VMEM((MOE_RUN * tm, D), F32),
                        pltpu.SemaphoreType.DMA((2,)), pltpu.SemaphoreType.DMA((1,))],
    )
    return pl.pallas_call(
        _moe_kernel,
        grid_spec=grid_spec,
        out_shape=jax.ShapeDtypeStruct((rows, D), F32),
        compiler_params=pltpu.CompilerParams(dimension_semantics=("arbitrary", "arbitrary"),
                                             vmem_limit_bytes=MOE_VMEM_LIMIT),
        name="moe_experts",
    )(unit_e, unit_blk, unit_len, n_units_used, tok_pad, hm, w_gu, w_gu, b_gu, b_gu, w_down, b_down)


def moe(hm, logits, w_gu, b_gu, w_down, b_down, layer):
    T, D = hm.shape
    E = logits.shape[-1]
    blk = MOE_TM
    top_val, top_idx = lax.top_k(logits, TOP_K)
    gates = jax.nn.softmax(top_val, axis=-1)
    TK = T * TOP_K
    flat_e = top_idx.reshape(TK).astype(jnp.int32)
    order = jnp.argsort(flat_e)
    e_sorted = flat_e[order]
    tok_sorted = (order // TOP_K).astype(jnp.int32)
    counts = jnp.bincount(flat_e, length=E).astype(jnp.int32)
    nblk_e = (counts + blk - 1) // blk
    blk_end = jnp.cumsum(nblk_e)
    blk_start = blk_end - nblk_e
    dest = blk_start[e_sorted] * blk + jnp.arange(TK, dtype=jnp.int32) - (jnp.cumsum(counts) - counts)[e_sorted]
    n_blocks = -(-(TK + E * (blk - 1)) // blk)
    rows = n_blocks * blk
    tok_pad = jnp.zeros((rows,), jnp.int32).at[dest].set(tok_sorted)
    nun_e = (nblk_e + MOE_RUN - 1) // MOE_RUN
    un_end = jnp.cumsum(nun_e)
    n_units = un_end[-1]
    max_units = -(-n_blocks // MOE_RUN) + E
    uid = jnp.minimum(jnp.arange(max_units, dtype=jnp.int32), n_units - 1)
    unit_e = jnp.minimum(jnp.searchsorted(un_end, uid, side='right'), E - 1).astype(jnp.int32)
    k_in_e = uid - (un_end - nun_e)[unit_e]
    unit_blk = (blk_start[unit_e] + k_in_e * MOE_RUN).astype(jnp.int32)
    unit_len = jnp.clip(nblk_e[unit_e] - k_in_e * MOE_RUN, 0, MOE_RUN)
    unit_len = jnp.where(jnp.arange(max_units) < n_units, unit_len, 0).astype(jnp.int32)
    n_used = jnp.stack([n_units, blk_end[-1]]).astype(jnp.int32)
    yb = moe_experts(unit_e, unit_blk, unit_len, n_used, tok_pad, hm, w_gu, b_gu, w_down, b_down, layer)
    pos = jnp.zeros((TK,), jnp.int32).at[order].set(dest)
    return yb, pos, gates


def kernel(x_prompt, x_sample, c_prompt, c_sample, state_rwkv, state_hgrn, state_shift, w_ada, b_ada, w_in, w_out, mu_rkv, mu_in, w0, w1, w2, a0, a1, a2, v0, v1, v2, mu_vg, g1, g2, k_k, k_a, r_k, lnx_g, lnx_b, hg_lower, hg_norm_w, ln1_g, ln1_b, ln2_g, ln2_b, w_router, b_router, w_gu, b_gu, w_down, b_down):
    Bp, Lp, D = x_prompt.shape
    Bs, Ls, _ = x_sample.shape
    depth = w_in.shape[0]
    d_a = w0.shape[1]
    rw_heads, rw_head = r_k.shape[1], r_k.shape[2]
    hg_dv = hg_norm_w.shape[1]
    nk = hg_lower.shape[1]
    d_b = D - d_a
    hg_heads = d_b // hg_dv
    hg_dk = nk // hg_heads
    nv = hg_heads * hg_dv
    n_exp = w_router.shape[-1]
    d_ff = w_down.shape[2]
    assert Lp % GROUP == 0 and Ls % GROUP == 0
    Tp, Ts = Bp * Lp, Bs * Ls
    T = Tp + Ts
    dn_alpha = (2 * depth) ** 0.25

    def split_groups(z):
        return z[:Tp].reshape(Bp, Lp, -1), z[Tp:].reshape(Bs, Ls, -1)

    def shift_tokens(z, prev_s):
        zp, zs = split_groups(z)
        zp = jnp.concatenate([jnp.zeros_like(zp[:, :1]), zp[:, :-1]], axis=1)
        zs = jnp.concatenate([prev_s[:, None], zs[:, :-1]], axis=1)
        return jnp.concatenate([zp.reshape(Tp, -1), zs.reshape(Ts, -1)])

    c_all = jax.nn.silu(jnp.concatenate([c_prompt, c_sample]))
    n_c = Bp + Bs
    c_pad = jnp.pad(c_all, ((0, (-n_c) % SUBLANES), (0, 0)))
    b_ada3 = b_ada[:, None, :]
    rep = jnp.concatenate([jnp.repeat(jnp.arange(Bp), Lp // GROUP), Bp + jnp.repeat(jnp.arange(Bs), Ls // GROUP)])

    p_lb = jax.nn.softmax(hg_lower.astype(F32), axis=0)
    lb_all3 = (jnp.cumsum(p_lb, axis=0) - p_lb[0])[:, None, :]
    hg_norm_w3 = hg_norm_w[:, None, :]
    assert nk == nv

    x = jnp.concatenate([x_prompt.reshape(Tp, D), x_sample.reshape(Ts, D)])
    ln1_g3, ln1_b3, ln2_g3, ln2_b3 = (t[:, None, :] for t in (ln1_g, ln1_b, ln2_g, ln2_b))
    b_router3 = b_router[:, None, :]
    b_gu4 = b_gu[:, :, None, :]
    b_down4 = b_down[:, :, None, :]

    v_first = None
    rw_p, rw_s, hg_p, hg_s, sh_p, sh_s = [], [], [], [], [], []
    for l in range(depth):
        mod_c = mm(c_pad, w_ada, l, b_ada3, name="ada")[:n_c]
        mod = mod_c[rep][:, None, :]
        SH1, SC1, GT1, SH2, SC2, GT2 = range(6)

        xm = modulate(x, mod, SC1, SH1)
        prev = state_shift[l]
        prev_bf = prev.astype(BF16)

        def last_rows(xg, m):
            sh1, sc1 = m[:, SH1 * D:(SH1 + 1) * D], m[:, SC1 * D:(SC1 + 1) * D]
            return xg[:, -1] * (1 + sc1) + sh1
        xp3, xs3 = split_groups(x)
        sh_p.append(last_rows(xp3, mod_c[:Bp]))
        sh_s.append(last_rows(xs3, mod_c[Bp:]))

        z = mm(xm, w_in, l, name="w_in")
        z_prev = mm(prev_bf, w_in, l, n_cols=3 * d_a, name="w_in_prev")

        lora_w = [w1[l], a1[l], g1[l]] + ([v1[l - 1]] if l > 0 else [])
        lora_mu = [mu_in[l, 0], mu_in[l, 1], mu_in[l, 2]] + ([mu_vg[l - 1]] if l > 0 else [])
        wl = jnp.concatenate(lora_w + [m[:, None] * w for m, w in zip(lora_mu, lora_w)], axis=1)
        nl = wl.shape[1] // 2
        zl = mm(xm, wl, name="lora_in")
        zl_prev = mm(prev_bf, wl, name="lora_in_prev")
        lora = zl[:, :nl] + shift_tokens(zl[:, nl:], zl_prev[:, nl:]) - zl[:, nl:]
        offs = [0]
        for w in lora_w:
            offs.append(offs[-1] + w.shape[1])
        lw, la, lg = (lora[:, offs[i]:offs[i + 1]] for i in range(3))

        zA = z[:, :3 * d_a]
        zA = zA + (shift_tokens(zA, z_prev) - zA) * mu_rkv[l]
        r, k, v = jnp.split(zA, 3, axis=-1)
        w_log = -jax.nn.softplus(-(w0[l] + mm(jnp.tanh(lw), w2, l, name="lora_w"))) - 0.5
        if l == 0:
            v_first = v
        else:
            lv = lora[:, offs[3]:offs[4]]
            v = v + (v_first - v) * jax.nn.sigmoid(v0[l - 1] + mm(lv, v2, l - 1, name="lora_v"))
        a = jax.nn.sigmoid(a0[l] + mm(la, a2, l, name="lora_a"))
        g = mm(jax.nn.sigmoid(lg), g2, l, name="lora_g")
        heads = lambda t: t.reshape(T, rw_heads, rw_head)
        kk = heads(k * k_k[l])
        kk = kk / jnp.maximum(jnp.sqrt(jnp.sum(kk * kk, -1, keepdims=True)), 1e-12)
        k = k * (1 + (a - 1) * k_a[l])
        decay = jnp.exp(-jnp.exp(w_log))
        a_vec = (-kk).reshape(T, d_a)
        b_vec = (kk * heads(a)).reshape(T, d_a)

        y_groups, s_groups = [], []
        for gi, (B, L, s0) in enumerate(((Bp, Lp, None), (Bs, Ls, state_rwkv[l]))):
            pick = lambda t: split_groups(t)[gi].reshape(B, L, rw_heads, rw_head)
            split = max(1, LANES // (B * rw_heads))
            ins = [chains_to_lanes(pick(t)) for t in (a_vec, decay, b_vec, k, r)]
            vl = vals_to_lanes(pick(v), split)
            vr = rw_head // split
            nlanes = B * rw_heads * split
            if s0 is None:
                s0l = jnp.zeros((vr, rw_head, nlanes), F32)
            else:
                s0l = jnp.transpose(s0.reshape(B, rw_heads, split, vr, rw_head), (3, 4, 2, 0, 1)).reshape(vr, rw_head, nlanes)
            yl, sl = rwkv_scan(*ins, vl, s0l)
            y_groups.append(vals_from_lanes(yl, B, rw_heads, split).reshape(B * L, d_a))
            s_groups.append(jnp.transpose(sl.reshape(vr, rw_head, split, B, rw_heads), (3, 4, 2, 0, 1)).reshape(B, rw_heads, rw_head, rw_head))
        rw_p.append(s_groups[0])
        rw_s.append(s_groups[1])
        y = heads(jnp.concatenate(y_groups))
        mu_y = jnp.mean(y, -1, keepdims=True)
        var_y = jnp.mean(jnp.square(y - mu_y), -1, keepdims=True)
        y = ((y - mu_y) * lax.rsqrt(var_y + RW_LN_EPS)).reshape(T, d_a) * lnx_g[l] + lnx_b[l]
        bonus = (jnp.sum(heads(r) * heads(k) * r_k[l], -1, keepdims=True) * heads(v)).reshape(T, d_a)
        oA = (y + bonus) * g

        oB_p, s_p = hgrn_mix(z, 3 * d_a, lb_all3, hg_norm_w3, l, None, Bp, Lp, 0, hg_heads, hg_dk, hg_dv)
        oB_s, s_s = hgrn_mix(z, 3 * d_a, lb_all3, hg_norm_w3, l, state_hgrn[l], Bs, Ls, Tp, hg_heads, hg_dk, hg_dv)
        hg_p.append(s_p)
        hg_s.append(s_s)
        oB = jnp.concatenate([oB_p, oB_s])

        mix = mm(jnp.concatenate([oA, oB], axis=-1).astype(BF16), w_out, l, name="w_out")
        h, hm, logits = ln_router(x, mix, mod, GT1, SC2, SH2, ln1_g3, ln1_b3, w_router, b_router3, l, dn_alpha)
        yb, pos, gates = moe(hm, logits, w_gu, b_gu4, w_down, b_down4, l)
        x = combine_ln(h, yb, pos, gates, mod, GT2, ln2_g3, ln2_b3, l, dn_alpha)

    y_prompt = x[:Tp].reshape(Bp, Lp, D)
    y_sample = x[Tp:].reshape(Bs, Ls, D)
    return (y_prompt, y_sample, jnp.stack(rw_p), jnp.stack(rw_s), jnp.stack(hg_p), jnp.stack(hg_s),
            jnp.stack(sh_p), jnp.stack(sh_s))
```
